```python
import math
import jax, jax.numpy as jnp
from jax import lax
import numpy as np

D_MODEL = 1024
BATCH = 2
SEQ = 8192
DEPTH = 4

GRID_W = 64
CTX_LEN = 256
HEAD_DIM = 64
D_MIX = D_MODEL
LRU_WIDTH = D_MIX // 4
LRU_BLOCKS = 4
LRU_BLOCK = LRU_WIDTH // LRU_BLOCKS
LRU_CONV = 4
LRU_C = 8.0
DIFF_WIDTH = D_MIX // 2
DIFF_VDIM = 2 * HEAD_DIM
DIFF_HEADS = DIFF_WIDTH // DIFF_VDIM
DIFF_QK = DIFF_HEADS * 2 * HEAD_DIM
Q_BLOCK = 128
NA_WIDTH = D_MIX - LRU_WIDTH - DIFF_WIDTH
NA_HEADS = NA_WIDTH // HEAD_DIM
NA_KH_MAX = 8
NA_KW = 16
IN_SECTIONS = (LRU_WIDTH, LRU_WIDTH, DIFF_QK, DIFF_QK, DIFF_WIDTH, NA_WIDTH, NA_WIDTH, NA_WIDTH)
IN_COLS = 2 * LRU_WIDTH + 2 * DIFF_QK + DIFF_WIDTH + 3 * NA_WIDTH
FFN_HIDDEN = -(-8 * D_MODEL // (3 * 256)) * 256
ROPE_BASE = 10000.0
NORM_EPS = 1e-6
NEG_INF = -1e30

kernel_name = 'hybrid_rglru_diffattn_natten_dit'


def _rmsnorm(x, g):
    xf = x.astype(jnp.float32)
    y = xf * lax.rsqrt(jnp.mean(xf * xf, axis=-1, keepdims=True) + NORM_EPS)
    return (y * g.astype(jnp.float32)).astype(x.dtype)


def _modulate(h, shift, scale):
    return h * (1 + scale[:, None, :]) + shift[:, None, :]


def _split_in(p):
    idx = []
    acc = 0
    for w in IN_SECTIONS[:-1]:
        acc += w
        idx.append(acc)
    return jnp.split(p, idx, axis=-1)


def _axial_rope_tables(n, dtype):
    t = jnp.arange(n, dtype=jnp.int32)
    row = (t // GRID_W).astype(jnp.float32)
    col = (t % GRID_W).astype(jnp.float32)
    axis_dim = HEAD_DIM // 2
    inv_freq = 1.0 / (ROPE_BASE ** (jnp.arange(0, axis_dim, 2, dtype=jnp.float32) / axis_dim))
    ang_r = row[:, None] * inv_freq[None]
    ang_c = col[:, None] * inv_freq[None]
    return (jnp.cos(ang_r).astype(dtype), jnp.sin(ang_r).astype(dtype),
            jnp.cos(ang_c).astype(dtype), jnp.sin(ang_c).astype(dtype))


def _rot(x, cos, sin):
    h = x.shape[-1] // 2
    x1, x2 = x[..., :h], x[..., h:]
    return jnp.concatenate([x1 * cos - x2 * sin, x2 * cos + x1 * sin], axis=-1)


def _apply_axial_rope(x, rope):
    cr, sr, cc, sc = rope
    shp = (1, cr.shape[0], 1, 1, cr.shape[1])
    half = HEAD_DIM // 2
    xr = _rot(x[..., :half], cr.reshape(shp), sr.reshape(shp))
    xc = _rot(x[..., half:], cc.reshape(shp), sc.reshape(shp))
    return jnp.concatenate([xr, xc], axis=-1)


def _dwconv_centred(x, w, b):
    L = x.shape[1]
    left = LRU_CONV // 2
    xp = jnp.pad(x, ((0, 0), (left, LRU_CONV - 1 - left), (0, 0)))
    y = b
    for j in range(LRU_CONV):
        y = y + w[j] * xp[:, j:j + L]
    return y


def _rglru_coeffs(xc, wa, ba, wx, bx, lam):
    B_, L, C = xc.shape
    xb = xc.reshape(B_, L, LRU_BLOCKS, LRU_BLOCK)
    r = jax.nn.sigmoid((jnp.einsum('blni,nij->blnj', xb, wa).reshape(B_, L, C) + ba).astype(jnp.float32))
    i = jax.nn.sigmoid((jnp.einsum('blni,nij->blnj', xb, wx).reshape(B_, L, C) + bx).astype(jnp.float32))
    log_a = LRU_C * r * jax.nn.log_sigmoid(lam.astype(jnp.float32))
    a = jnp.exp(log_a)
    b = jnp.sqrt(-jnp.expm1(2.0 * log_a)) * (i * xc.astype(jnp.float32))
    return a, b


def _linear_scan(a, b, h0):
    b = b.at[:, 0].add(a[:, 0] * h0)

    def comb(x, y):
        return (x[0] * y[0], y[0] * x[1] + y[1])

    return lax.associative_scan(comb, (a, b), axis=1)[1]


def _bidir_rglru(xl, xc, wa, ba, wx, bx, lam, want_ctx):
    zeros = jnp.zeros((xc.shape[0], xc.shape[2]), jnp.float32)
    a, b = _rglru_coeffs(xc, wa[0], ba[0], wx[0], bx[0], lam[0])
    hc_f = _linear_scan(a, b, zeros)
    a, b = _rglru_coeffs(xl, wa[0], ba[0], wx[0], bx[0], lam[0])
    hl_f = _linear_scan(a, b, hc_f[:, -1])
    a, b = _rglru_coeffs(xc, wa[1], ba[1], wx[1], bx[1], lam[1])
    hc_b = jnp.flip(_linear_scan(jnp.flip(a, 1), jnp.flip(b, 1), zeros), 1)
    a, b = _rglru_coeffs(xl, wa[1], ba[1], wx[1], bx[1], lam[1])
    hl_b = jnp.flip(_linear_scan(jnp.flip(a, 1), jnp.flip(b, 1), hc_b[:, 0]), 1)
    y_ctx = (hc_f + hc_b) if want_ctx else None
    return hl_f + hl_b, y_ctx


def _diff_attend(q, keys, vals, lam):
    s = jnp.einsum('bqhcd,bkhcd->bhcqk', q, keys, preferred_element_type=jnp.float32) * (HEAD_DIM ** -0.5)
    p = jax.nn.softmax(s, axis=-1)
    w = p[:, :, 0] - lam * p[:, :, 1]
    return jnp.einsum('bhqk,bkhe->bqhe', w.astype(vals.dtype), vals)


def _plain_attend(q, k, v):
    s = jnp.einsum('bqhd,bkhd->bhqk', q, k, preferred_element_type=jnp.float32) * (q.shape[-1] ** -0.5)
    p = jax.nn.softmax(s, axis=-1).astype(v.dtype)
    return jnp.einsum('bhqk,bkhd->bqhd', p, v)


def _na_latent(q, k, v, kc, vc, rpb):
    B_, S, H, d = q.shape
    rows = S // GRID_W
    kh = min(NA_KH_MAX, rows)
    qg = q.reshape(B_, rows, GRID_W, H, d)
    kg = k.reshape(B_, rows, GRID_W, H, d)
    vg = v.reshape(B_, rows, GRID_W, H, d)
    r = jnp.arange(rows)
    r0 = jnp.clip(r - kh // 2, 0, rows - kh)
    row_idx = r0[:, None] + jnp.arange(kh)[None]
    k_strip = kg[:, row_idx]
    v_strip = vg[:, row_idx]
    cq = jnp.arange(GRID_W)
    c0 = jnp.clip(cq - NA_KW // 2, 0, GRID_W - NA_KW)
    in_win = (cq[None] >= c0[:, None]) & (cq[None] < c0[:, None] + NA_KW)
    roff = row_idx - r[:, None] + (NA_KH_MAX - 1)
    coff = jnp.clip(cq[None] - cq[:, None] + (NA_KW - 1), 0, 2 * NA_KW - 2)
    bias = rpb[:, roff[:, None, :, None], coff[None, :, None, :]].astype(jnp.float32)
    bias = jnp.where(in_win[None, None, :, None, :], bias, NEG_INF)
    scale = d ** -0.5
    s_loc = jnp.einsum('brqhd,brkwhd->bhrqkw', qg, k_strip, preferred_element_type=jnp.float32) * scale + bias[None]
    n_loc = kh * GRID_W
    s_loc = s_loc.reshape(B_, H, rows, GRID_W, n_loc)
    s_ctx = jnp.einsum('brqhd,bchd->bhrqc', qg, kc, preferred_element_type=jnp.float32) * scale
    p = jax.nn.softmax(jnp.concatenate([s_loc, s_ctx], axis=-1), axis=-1).astype(v.dtype)
    p_loc = p[..., :n_loc].reshape(B_, H, rows, GRID_W, kh, GRID_W)
    out = (jnp.einsum('bhrqkw,brkwhd->brqhd', p_loc, v_strip)
           + jnp.einsum('bhrqc,bchd->brqhd', p[..., n_loc:], vc))
    return out.reshape(B_, S, H * d)


def _mixer(px, pc, lam_init, want_ctx, rope, conv_w, conv_b, wa, ba, wx, bx, lru_lam, diff_lam, subln_g, rpb):
    B_, S, _ = px.shape
    Bc, C, _ = pc.shape
    xl, gl, qd, kd, vd, qn, kn, vn = _split_in(px)
    cxl, cgl, cqd, ckd, cvd, cqn, ckn, cvn = _split_in(pc)
    y_lat, y_ctx = _bidir_rglru(_dwconv_centred(xl, conv_w, conv_b), _dwconv_centred(cxl, conv_w, conv_b),
                                wa, ba, wx, bx, lru_lam, want_ctx)
    lru_x = y_lat.astype(px.dtype) * jax.nn.gelu(gl)
    dl = diff_lam.astype(jnp.float32)
    lam = jnp.exp(jnp.sum(dl[0] * dl[1])) - jnp.exp(jnp.sum(dl[2] * dl[3])) + lam_init
    qd = _apply_axial_rope(qd.reshape(B_, S, DIFF_HEADS, 2, HEAD_DIM), rope)
    kd = _apply_axial_rope(kd.reshape(B_, S, DIFF_HEADS, 2, HEAD_DIM), rope)
    vd = vd.reshape(B_, S, DIFF_HEADS, DIFF_VDIM)
    ckd = ckd.reshape(Bc, C, DIFF_HEADS, 2, HEAD_DIM)
    cvd = cvd.reshape(Bc, C, DIFF_HEADS, DIFF_VDIM)
    keys = jnp.concatenate([ckd, kd], axis=1)
    vals = jnp.concatenate([cvd, vd], axis=1)
    nb = S // Q_BLOCK
    qblocks = jnp.moveaxis(qd.reshape(B_, nb, Q_BLOCK, DIFF_HEADS, 2, HEAD_DIM), 1, 0)
    od = lax.map(lambda qb: _diff_attend(qb, keys, vals, lam), qblocks)
    od = jnp.moveaxis(od, 0, 1).reshape(B_, S, DIFF_HEADS, DIFF_VDIM)
    diff_x = (_rmsnorm(od, subln_g) * (1.0 - lam_init)).reshape(B_, S, DIFF_WIDTH)
    ckn = ckn.reshape(Bc, C, NA_HEADS, HEAD_DIM)
    cvn = cvn.reshape(Bc, C, NA_HEADS, HEAD_DIM)
    na_x = _na_latent(qn.reshape(B_, S, NA_HEADS, HEAD_DIM), kn.reshape(B_, S, NA_HEADS, HEAD_DIM),
                      vn.reshape(B_, S, NA_HEADS, HEAD_DIM), ckn, cvn, rpb)
    mix_x = jnp.concatenate([lru_x, diff_x, na_x], axis=-1)
    if not want_ctx:
        return mix_x, None
    lru_c = y_ctx.astype(pc.dtype) * jax.nn.gelu(cgl)
    od_c = _diff_attend(cqd.reshape(Bc, C, DIFF_HEADS, 2, HEAD_DIM), ckd, cvd, lam)
    diff_c = (_rmsnorm(od_c, subln_g) * (1.0 - lam_init)).reshape(Bc, C, DIFF_WIDTH)
    na_c = _plain_attend(cqn.reshape(Bc, C, NA_HEADS, HEAD_DIM), ckn, cvn).reshape(Bc, C, NA_WIDTH)
    mix_c = jnp.concatenate([lru_c, diff_c, na_c], axis=-1)
    return mix_x, mix_c


def _swiglu(h, w_gu, w_down):
    g, u = jnp.split(h @ w_gu, 2, axis=-1)
    return (jax.nn.silu(g) * u) @ w_down


def setup_inputs(seed: int = 0) -> dict:
    key = jax.random.key(seed)
    ks = jax.random.split(key, 24)
    f32 = jnp.float32
    D = D_MODEL

    def nrm(k, shape, scale):
        return jax.random.normal(k, shape, f32) * scale

    u = jax.random.uniform(ks[15], (DEPTH, 2, LRU_WIDTH), f32, 0.9, 0.999)
    sa = u ** (1.0 / LRU_C)
    return {
        'x': nrm(ks[0], (BATCH, SEQ, D), 1.0),
        'c': nrm(ks[1], (BATCH, D), 1.0),
        'ctx': nrm(ks[2], (BATCH, CTX_LEN, D), 1.0),
        'c_ctx': nrm(ks[3], (D,), 1.0),
        'w_mod': nrm(ks[4], (DEPTH, D, 6 * D), 0.5 * D ** -0.5),
        'b_mod': nrm(ks[5], (DEPTH, 6 * D), 0.01),
        'norm1_g': 1.0 + nrm(ks[6], (DEPTH, D), 0.02),
        'norm2_g': 1.0 + nrm(ks[7], (DEPTH, D), 0.02),
        'w_in': nrm(ks[8], (DEPTH, D, IN_COLS), D ** -0.5),
        'lru_conv_w': nrm(ks[9], (DEPTH, LRU_CONV, LRU_WIDTH), LRU_CONV ** -0.5),
        'lru_conv_b': nrm(ks[10], (DEPTH, LRU_WIDTH), 0.01),
        'lru_wa': nrm(ks[11], (DEPTH, 2, LRU_BLOCKS, LRU_BLOCK, LRU_BLOCK), LRU_BLOCK ** -0.5),
        'lru_ba': nrm(ks[12], (DEPTH, 2, LRU_WIDTH), 0.01),
        'lru_wx': nrm(ks[13], (DEPTH, 2, LRU_BLOCKS, LRU_BLOCK, LRU_BLOCK), LRU_BLOCK ** -0.5),
        'lru_bx': nrm(ks[14], (DEPTH, 2, LRU_WIDTH), 0.01),
        'lru_lambda': jnp.log(sa) - jnp.log1p(-sa),
        'diff_lambda': nrm(ks[16], (DEPTH, 4, HEAD_DIM), 0.1),
        'diff_subln_g': 1.0 + nrm(ks[17], (DEPTH, DIFF_VDIM), 0.02),
        'na_rpb': nrm(ks[18], (DEPTH, NA_HEADS, 2 * NA_KH_MAX - 1, 2 * NA_KW - 1), 0.1),
        'w_out': nrm(ks[19], (DEPTH, D_MIX, D), D_MIX ** -0.5),
        'ffn_w_gu': nrm(ks[20], (DEPTH, D, 2 * FFN_HIDDEN), D ** -0.5),
        'ffn_w_down': nrm(ks[21], (DEPTH, FFN_HIDDEN, D), FFN_HIDDEN ** -0.5),
        'final_g': 1.0 + nrm(ks[22], (D,), 0.02),
    }


def reference(x, c, ctx, c_ctx, w_mod, b_mod, norm1_g, norm2_g, w_in, lru_conv_w, lru_conv_b,
              lru_wa, lru_ba, lru_wx, lru_bx, lru_lambda, diff_lambda, diff_subln_g, na_rpb,
              w_out, ffn_w_gu, ffn_w_down, final_g):
    S = x.shape[1]
    rope = _axial_rope_tables(S, x.dtype)
    cond_x = jax.nn.silu(c)
    cond_c = jax.nn.silu(c_ctx)[None]
    h_ctx = ctx
    for l in range(DEPTH):
        last = l == DEPTH - 1
        lam_init = 0.8 - 0.6 * math.exp(-0.3 * l)
        mx = jnp.split(cond_x @ w_mod[l] + b_mod[l], 6, axis=-1)
        mc = jnp.split(cond_c @ w_mod[l] + b_mod[l], 6, axis=-1)
        hx = _modulate(_rmsnorm(x, norm1_g[l]), mx[0], mx[1])
        hc = _modulate(_rmsnorm(h_ctx, norm1_g[l]), mc[0], mc[1])
        mix_x, mix_c = _mixer(hx @ w_in[l], hc @ w_in[l], lam_init, not last, rope,
                              lru_conv_w[l], lru_conv_b[l], lru_wa[l], lru_ba[l], lru_wx[l], lru_bx[l],
                              lru_lambda[l], diff_lambda[l], diff_subln_g[l], na_rpb[l])
        x = x + mx[2][:, None, :] * (mix_x @ w_out[l])
        hx2 = _modulate(_rmsnorm(x, norm2_g[l]), mx[3], mx[4])
        x = x + mx[5][:, None, :] * _swiglu(hx2, ffn_w_gu[l], ffn_w_down[l])
        if not last:
            h_ctx = h_ctx + mc[2][:, None, :] * (mix_c @ w_out[l])
            hc2 = _modulate(_rmsnorm(h_ctx, norm2_g[l]), mc[3], mc[4])
            h_ctx = h_ctx + mc[5][:, None, :] * _swiglu(hc2, ffn_w_gu[l], ffn_w_down[l])
    return _rmsnorm(x, final_g)
```

```python
import functools
import math

import jax
import jax.numpy as jnp
from jax import lax
from jax.experimental import pallas as pl
from jax.experimental.pallas import tpu as pltpu

GRID_W = 64
HEAD_DIM = 64
LRU_BLOCKS = 4
LRU_CONV = 4
LRU_C = 8.0
NA_KH = 8
NA_KW = 16
ROPE_BASE = 10000.0
NORM_EPS = 1e-6
NEG_INF = -1e30
CONV_HALO = 8
VMEM_LIMIT = 56 * 1024 * 1024

F32 = jnp.float32
BF16 = jnp.bfloat16


def _cparams(sem):
    return pltpu.CompilerParams(dimension_semantics=sem, vmem_limit_bytes=VMEM_LIMIT)


def _full(shape):
    nd = len(shape)
    return pl.BlockSpec(shape, lambda *_: (0,) * nd, pipeline_mode=pl.Buffered(1))


def _rms(x, g):
    return x * lax.rsqrt(jnp.mean(x * x, axis=-1, keepdims=True) + NORM_EPS) * g


def _sigmoid(x):
    return 1.0 / (1.0 + jnp.exp(-x))


def _dot_t(a, b):
    return lax.dot_general(a, b, (((1,), (1,)), ((), ())), preferred_element_type=F32)


def _mod_kernel(c_ref, w_ref, b_ref, o_ref):
    cv = c_ref[...]
    cond = cv * _sigmoid(cv)
    o_ref[...] = jnp.dot(cond, w_ref[...], preferred_element_type=F32,
                         precision=lax.Precision.HIGHEST) + b_ref[...]


def _mod_call(cvec, w_mod, b_mod):
    depth, d, n = w_mod.shape
    rows = cvec.shape[0]
    tn = d
    return pl.pallas_call(
        _mod_kernel,
        grid=(depth, n // tn),
        in_specs=[pl.BlockSpec((rows, d), lambda l, j: (0, 0)),
                  pl.BlockSpec((None, d, tn), lambda l, j: (l, 0, j)),
                  pl.BlockSpec((None, 1, tn), lambda l, j: (l, 0, j))],
        out_specs=pl.BlockSpec((None, rows, tn), lambda l, j: (l, 0, j)),
        out_shape=jax.ShapeDtypeStruct((depth, rows, n), F32),
        compiler_params=_cparams(("arbitrary", "arbitrary")),
        name="mod",
    )(cvec, w_mod, b_mod.reshape(depth, 1, n))


def _rope(p, cos, sin_signed):
    n = p.shape[-1]
    lane = lax.broadcasted_iota(jnp.int32, p.shape, 1)
    low = (lane % 32) < 16
    partner = jnp.where(low, pltpu.roll(p, n - 16, 1), pltpu.roll(p, 16, 1))
    return p * cos + partner * sin_signed


def _in_proj_kernel(*refs, sections, rope, q_scale):
    if rope:
        x_ref, g_ref, sh_ref, sc_ref, w_ref, cos_ref, sin_ref = refs[:7]
        outs = refs[7:]
    else:
        x_ref, g_ref, sh_ref, sc_ref, w_ref = refs[:5]
        outs = refs[5:]
    h = _rms(x_ref[...], g_ref[...]) * (1.0 + sc_ref[...]) + sh_ref[...]
    hb = h.astype(BF16)
    off = 0
    for idx, (width, o_ref) in enumerate(zip(sections, outs)):
        p = jnp.dot(hb, w_ref[:, off:off + width], preferred_element_type=F32)
        if rope and idx in (2, 3):
            reps = width // cos_ref.shape[-1]
            cos = jnp.concatenate([cos_ref[...]] * reps, axis=-1)
            sin = jnp.concatenate([sin_ref[...]] * reps, axis=-1)
            p = _rope(p, cos, sin)
        if idx in (2, 5):
            p = p * q_scale
        o_ref[...] = p.astype(o_ref.dtype)
        off += width


def _in_proj_call(xs, g, shift, scale, w, rope_tabs, *, sections, rows_per_batch, tm):
    t, d = xs.shape
    n_mod = shift.shape[0]
    per_batch = rows_per_batch // tm
    mod_idx = (lambda i: (i // per_batch, 0, 0)) if n_mod > 1 else (lambda i: (0, 0, 0))
    in_specs = [pl.BlockSpec((tm, d), lambda i: (i, 0)),
                _full((1, d)),
                pl.BlockSpec((None, 1, d), mod_idx),
                pl.BlockSpec((None, 1, d), mod_idx),
                _full(w.shape)]
    args = [xs, g.reshape(1, d), shift, scale, w]
    rope = rope_tabs is not None
    if rope:
        cos, sin = rope_tabs
        tw = cos.shape[-1]
        in_specs += [pl.BlockSpec((tm, tw), lambda i: (i % per_batch, 0))] * 2
        args += [cos, sin]
    dtypes = (F32, F32, BF16, BF16, BF16, BF16, BF16, BF16)
    out_shape = tuple(jax.ShapeDtypeStruct((t, wd), dt) for wd, dt in zip(sections, dtypes))
    out_specs = tuple(pl.BlockSpec((tm, wd), lambda i: (i, 0)) for wd in sections)
    return pl.pallas_call(
        functools.partial(_in_proj_kernel, sections=sections, rope=rope, q_scale=HEAD_DIM ** -0.5),
        grid=(t // tm,),
        in_specs=in_specs,
        out_specs=out_specs,
        out_shape=out_shape,
        compiler_params=_cparams(("arbitrary",)),
        name="in_proj_rope" if rope else "in_proj",
    )(*args)


def _lru_kernel(*refs, reverse, combine, tc, n_chunks):
    if combine:
        (prev_ref, x_ref, next_ref, cw_ref, cb_ref, w_ref, bias_ref, lam_ref, h0_ref,
         hf_ref, gate_ref, h_out_ref, y_out_ref, carry_ref) = refs
    else:
        (prev_ref, x_ref, next_ref, cw_ref, cb_ref, w_ref, bias_ref, lam_ref, h0_ref,
         h_out_ref, carry_ref) = refs
    j = pl.program_id(1)
    chunk = (n_chunks - 1 - j) if reverse else j

    @pl.when(j == 0)
    def _():
        carry_ref[...] = h0_ref[...]

    width = x_ref.shape[-1]
    has_prev = (chunk > 0).astype(F32)
    has_next = (chunk < n_chunks - 1).astype(F32)
    xcat = jnp.concatenate([prev_ref[...] * has_prev, x_ref[...], next_ref[...] * has_next], axis=0)
    left = LRU_CONV // 2
    u = cb_ref[...]
    for tap in range(LRU_CONV):
        start = CONV_HALO - left + tap
        u = u + cw_ref[tap:tap + 1, :] * xcat[start:start + tc, :]

    gates = jnp.dot(u.astype(BF16), w_ref[...], preferred_element_type=F32) + bias_ref[...]
    r = _sigmoid(gates[:, :width])
    i = _sigmoid(gates[:, width:])
    lam = lam_ref[...]
    log_sig = jnp.minimum(lam, 0.0) - jnp.log(1.0 + jnp.exp(-jnp.abs(lam)))
    log_a = LRU_C * r * log_sig
    a = jnp.exp(log_a)
    b = jnp.sqrt(1.0 - jnp.exp(2.0 * log_a)) * (i * u)

    t_idx = lax.broadcasted_iota(jnp.int32, (tc, width), 0)
    d = 1
    while d < tc:
        if reverse:
            valid = t_idx < tc - d
            shift = tc - d
        else:
            valid = t_idx >= d
            shift = d
        a_sh = jnp.where(valid, pltpu.roll(a, shift, 0), 1.0)
        b_sh = jnp.where(valid, pltpu.roll(b, shift, 0), 0.0)
        b = a * b_sh + b
        a = a * a_sh
        d *= 2
    h = b + a * carry_ref[...]
    carry_ref[...] = h[0:1, :] if reverse else h[tc - 1:tc, :]
    h_out_ref[...] = h
    if combine:
        g = gate_ref[...]
        gelu = 0.5 * g * (1.0 + jnp.tanh(math.sqrt(2.0 / math.pi) * (g + 0.044715 * (g * g * g))))
        y_out_ref[...] = ((hf_ref[...] + h) * gelu).astype(y_out_ref.dtype)


def _lru_call(xl, conv_w, conv_b, w_gates, b_gates, lam, h0, hf=None, gate=None, *,
              seq, reverse, tc):
    t, width = xl.shape
    batch = t // seq
    n_chunks = seq // tc
    hb = tc // CONV_HALO
    n_halo = seq // CONV_HALO
    combine = hf is not None

    def chunk_of(j):
        return (n_chunks - 1 - j) if reverse else j

    def main_idx(b, j):
        return (b * n_chunks + chunk_of(j), 0)

    def prev_idx(b, j):
        return (b * n_halo + jnp.maximum(chunk_of(j) * hb - 1, 0), 0)

    def next_idx(b, j):
        return (b * n_halo + jnp.minimum((chunk_of(j) + 1) * hb, n_halo - 1), 0)

    in_specs = [pl.BlockSpec((CONV_HALO, width), prev_idx),
                pl.BlockSpec((tc, width), main_idx),
                pl.BlockSpec((CONV_HALO, width), next_idx),
                _full(conv_w.shape), _full((1, width)), _full(w_gates.shape),
                _full((1, 2 * width)), _full((1, width)),
                pl.BlockSpec((None, 1, width), lambda b, j: (b, 0, 0))]
    args = [xl, xl, xl, conv_w, conv_b.reshape(1, width), w_gates, b_gates.reshape(1, 2 * width),
            lam.reshape(1, width), h0.reshape(batch, 1, width)]
    out_shape = [jax.ShapeDtypeStruct((t, width), F32)]
    out_specs = [pl.BlockSpec((tc, width), main_idx)]
    if combine:
        in_specs += [pl.BlockSpec((tc, width), main_idx)] * 2
        args += [hf, gate]
        out_shape.append(jax.ShapeDtypeStruct((t, width), BF16))
        out_specs.append(pl.BlockSpec((tc, width), main_idx))
    res = pl.pallas_call(
        functools.partial(_lru_kernel, reverse=reverse, combine=combine, tc=tc, n_chunks=n_chunks),
        grid=(batch, n_chunks),
        in_specs=in_specs,
        out_specs=tuple(out_specs),
        out_shape=tuple(out_shape),
        scratch_shapes=[pltpu.VMEM((1, width), F32)],
        compiler_params=_cparams(("arbitrary", "arbitrary")),
        name="lru_bwd" if reverse else "lru_fwd",
    )(*args)
    return res if combine else res[0]


def _diff_lambda(dl_ref, lam_init):
    dl = dl_ref[...]
    s01 = jnp.sum(dl[0:1, :] * dl[1:2, :], axis=-1, keepdims=True)
    s23 = jnp.sum(dl[2:3, :] * dl[3:4, :], axis=-1, keepdims=True)
    return jnp.exp(s01) - jnp.exp(s23) + lam_init


def _diff_attn_kernel(q_ref, k_ref, v_ref, dl_ref, g_ref, o_ref, *, tq, tk, n_kt, lam_init):
    q = q_ref[...]
    lane = lax.broadcasted_iota(jnp.int32, q.shape, 1)
    zero = jnp.zeros_like(q)
    qs = jnp.concatenate([jnp.where(lane < HEAD_DIM, q, zero), jnp.where(lane >= HEAD_DIM, q, zero)], axis=0)

    def body(j, carry):
        m, l, acc = carry
        start = pl.multiple_of(j * tk, tk)
        kt = k_ref[pl.ds(start, tk), :]
        vt = v_ref[pl.ds(start, tk), :]
        s = _dot_t(qs, kt)
        m_new = jnp.maximum(m, jnp.max(s, axis=-1, keepdims=True))
        alpha = jnp.exp(m - m_new)
        p = jnp.exp(s - m_new)
        l = alpha * l + jnp.sum(p, axis=-1, keepdims=True)
        acc = alpha * acc + jnp.dot(p.astype(BF16), vt, preferred_element_type=F32)
        return m_new, l, acc

    init = (jnp.full((2 * tq, 1), -jnp.inf, F32), jnp.zeros((2 * tq, 1), F32),
            jnp.zeros((2 * tq, v_ref.shape[-1]), F32))
    _, l, acc = lax.fori_loop(0, n_kt, body, init)
    o = acc / l
    lam = _diff_lambda(dl_ref, lam_init)
    od = o[:tq, :] - lam * o[tq:, :]
    o_ref[...] = (_rms(od, g_ref[...]) * (1.0 - lam_init)).astype(o_ref.dtype)


def _diff_attn_call(q, k, v, dl, g, *, seq, n_keys, tq, tk, lam_init):
    t, width = q.shape
    batch = t // seq
    hw = 2 * HEAD_DIM
    heads = width // hw
    nq = seq // tq
    return pl.pallas_call(
        functools.partial(_diff_attn_kernel, tq=tq, tk=tk, n_kt=n_keys // tk, lam_init=lam_init),
        grid=(batch, heads, nq),
        in_specs=[pl.BlockSpec((tq, hw), lambda b, h, i: (b * nq + i, h)),
                  pl.BlockSpec((None, n_keys, hw), lambda b, h, i: (b, 0, h)),
                  pl.BlockSpec((None, n_keys, hw), lambda b, h, i: (b, 0, h)),
                  _full(dl.shape), _full((1, hw))],
        out_specs=pl.BlockSpec((tq, hw), lambda b, h, i: (b * nq + i, h)),
        out_shape=jax.ShapeDtypeStruct((t, width), BF16),
        compiler_params=_cparams(("arbitrary", "arbitrary", "arbitrary")),
        name="diff_attn",
    )(q, k, v, dl, g.reshape(1, hw))


def _rpb_table_kernel(r_ref, o_ref):
    n = o_ref.shape[-1]
    j = lax.broadcasted_iota(jnp.int32, (r_ref.shape[-1], n), 0)
    pos = lax.broadcasted_iota(jnp.int32, (r_ref.shape[-1], n), 1)
    qc = pos // GRID_W
    kc = pos % GRID_W
    coff = jnp.clip(kc - qc + (NA_KW - 1), 0, 2 * NA_KW - 2)
    onehot = (coff == j).astype(F32)
    vals = jnp.dot(r_ref[...], onehot, preferred_element_type=F32, precision=lax.Precision.HIGHEST)
    pos1 = lax.broadcasted_iota(jnp.int32, vals.shape, 1)
    qc1 = pos1 // GRID_W
    kc1 = pos1 % GRID_W
    c0 = jnp.clip(qc1 - NA_KW // 2, 0, GRID_W - NA_KW)
    in_win = (kc1 >= c0) & (kc1 < c0 + NA_KW)
    o_ref[...] = jnp.where(in_win, vals, NEG_INF)


def _rpb_table(rpb):
    heads = rpb.shape[0]
    ncol = 2 * NA_KW
    rp = jnp.pad(rpb, ((0, 0), (0, 0), (0, ncol - rpb.shape[-1])))
    r_all = jnp.stack([rp[:, NA_KH - 1 - p:2 * NA_KH - 1 - p, :] for p in range(NA_KH)], axis=0)
    r_all = r_all.reshape(NA_KH, heads * NA_KH, ncol)
    n = GRID_W * GRID_W
    tab = pl.pallas_call(
        _rpb_table_kernel,
        grid=(NA_KH,),
        in_specs=[pl.BlockSpec((None, heads * NA_KH, ncol), lambda p: (p, 0, 0))],
        out_specs=pl.BlockSpec((None, heads * NA_KH, n), lambda p: (p, 0, 0)),
        out_shape=jax.ShapeDtypeStruct((NA_KH, heads * NA_KH, n), F32),
        compiler_params=_cparams(("arbitrary",)),
        name="rpb_table",
    )(r_all)
    tab = tab.reshape(NA_KH, heads, NA_KH, GRID_W, GRID_W).transpose(0, 1, 3, 2, 4)
    return tab.reshape(NA_KH, heads * GRID_W, NA_KH * GRID_W)


def _stack_heads(q, heads):
    head_of_lane = lax.broadcasted_iota(jnp.int32, q.shape, 1) // HEAD_DIM
    zero = jnp.zeros_like(q)
    return jnp.concatenate([jnp.where(head_of_lane == h, q, zero) for h in range(heads)], axis=0)


def _unstack_heads(res, heads):
    n = res.shape[0] // heads
    head_of_lane = lax.broadcasted_iota(jnp.int32, (n, res.shape[1]), 1) // HEAD_DIM
    out = jnp.zeros((n, res.shape[1]), res.dtype)
    for h in range(heads):
        out = jnp.where(head_of_lane == h, res[h * n:(h + 1) * n, :], out)
    return out


def _na_kernel(q_ref, k_ref, v_ref, kc_ref, vc_ref, bias_ref, o_ref, *, rows, heads):
    r = pl.program_id(1)
    r0 = jnp.clip(r - NA_KH // 2, 0, rows - NA_KH)
    pat = r - r0
    n_loc = NA_KH * GRID_W
    start = pl.multiple_of(r0 * GRID_W, GRID_W)
    ks = k_ref[pl.ds(start, n_loc), :]
    vs = v_ref[pl.ds(start, n_loc), :]
    qs = _stack_heads(q_ref[...], heads)
    s_loc = _dot_t(qs, ks) + bias_ref[pat]
    s_ctx = _dot_t(qs, kc_ref[...])
    m = jnp.maximum(jnp.max(s_loc, axis=-1, keepdims=True), jnp.max(s_ctx, axis=-1, keepdims=True))
    e_loc = jnp.exp(s_loc - m)
    e_ctx = jnp.exp(s_ctx - m)
    denom = jnp.sum(e_loc, axis=-1, keepdims=True) + jnp.sum(e_ctx, axis=-1, keepdims=True)
    res = (jnp.dot(e_loc.astype(BF16), vs, preferred_element_type=F32)
           + jnp.dot(e_ctx.astype(BF16), vc_ref[...], preferred_element_type=F32)) / denom
    o_ref[...] = _unstack_heads(res, heads).astype(o_ref.dtype)


def _na_call(q, k, v, kc, vc, bias, *, seq):
    t, width = q.shape
    batch = t // seq
    rows = seq // GRID_W
    heads = width // HEAD_DIM
    n_ctx = kc.shape[1]
    return pl.pallas_call(
        functools.partial(_na_kernel, rows=rows, heads=heads),
        grid=(batch, rows),
        in_specs=[pl.BlockSpec((GRID_W, width), lambda b, r: (b * rows + r, 0)),
                  pl.BlockSpec((None, seq, width), lambda b, r: (b, 0, 0)),
                  pl.BlockSpec((None, seq, width), lambda b, r: (b, 0, 0)),
                  pl.BlockSpec((None, n_ctx, width), lambda b, r: (b, 0, 0)),
                  pl.BlockSpec((None, n_ctx, width), lambda b, r: (b, 0, 0)),
                  _full(bias.shape)],
        out_specs=pl.BlockSpec((GRID_W, width), lambda b, r: (b * rows + r, 0)),
        out_shape=jax.ShapeDtypeStruct((t, width), BF16),
        compiler_params=_cparams(("arbitrary", "arbitrary")),
        name="na_attn",
    )(q, k, v, kc, vc, bias)


def _plain_attn_kernel(q_ref, k_ref, v_ref, o_ref, *, heads):
    qs = _stack_heads(q_ref[...], heads)
    s = _dot_t(qs, k_ref[...])
    e = jnp.exp(s - jnp.max(s, axis=-1, keepdims=True))
    res = jnp.dot(e.astype(BF16), v_ref[...], preferred_element_type=F32) / jnp.sum(e, axis=-1, keepdims=True)
    o_ref[...] = _unstack_heads(res, heads).astype(o_ref.dtype)


def _plain_attn_call(q, k, v, *, seq):
    t, width = q.shape
    spec = pl.BlockSpec((seq, width), lambda b: (b, 0))
    return pl.pallas_call(
        functools.partial(_plain_attn_kernel, heads=width // HEAD_DIM),
        grid=(t // seq,),
        in_specs=[spec, spec, spec],
        out_specs=spec,
        out_shape=jax.ShapeDtypeStruct((t, width), BF16),
        compiler_params=_cparams(("arbitrary",)),
        name="ctx_plain_attn",
    )(q, k, v)


def _out_ffn_kernel(*refs, widths, hidden, h_chunk, final):
    (x_ref, lru_ref, diff_ref, na_ref, wo_ref, g1_ref, n2_ref, sh_ref, sc_ref, g2_ref,
     wgu_ref, wd_ref) = refs[:12]
    if final:
        fg_ref, o_ref = refs[12:]
    else:
        o_ref = refs[12]
    off = 0
    mix = None
    for part_ref, wd_ in zip((lru_ref, diff_ref, na_ref), widths):
        term = jnp.dot(part_ref[...], wo_ref[off:off + wd_, :], preferred_element_type=F32)
        mix = term if mix is None else mix + term
        off += wd_
    x1 = x_ref[...] + g1_ref[...] * mix
    hb = (_rms(x1, n2_ref[...]) * (1.0 + sc_ref[...]) + sh_ref[...]).astype(BF16)
    acc = None
    for c0 in range(0, hidden, h_chunk):
        g = jnp.dot(hb, wgu_ref[:, c0:c0 + h_chunk], preferred_element_type=F32)
        u = jnp.dot(hb, wgu_ref[:, hidden + c0:hidden + c0 + h_chunk], preferred_element_type=F32)
        a = (g * _sigmoid(g) * u).astype(BF16)
        term = jnp.dot(a, wd_ref[c0:c0 + h_chunk, :], preferred_element_type=F32)
        acc = term if acc is None else acc + term
    x2 = x1 + g2_ref[...] * acc
    if final:
        x2 = _rms(x2, fg_ref[...])
    o_ref[...] = x2


def _out_ffn_call(xs, lru, diff, na, w_out, gate1, n2g, shift, scale, gate2, w_gu, w_down, final_g,
                  *, rows_per_batch, tm, h_chunk):
    t, d = xs.shape
    hidden = w_down.shape[0]
    n_mod = shift.shape[0]
    per_batch = rows_per_batch // tm
    mod_idx = (lambda i: (i // per_batch, 0, 0)) if n_mod > 1 else (lambda i: (0, 0, 0))
    mod_spec = pl.BlockSpec((None, 1, d), mod_idx)
    row = lambda wd_: pl.BlockSpec((tm, wd_), lambda i: (i, 0))
    widths = (lru.shape[1], diff.shape[1], na.shape[1])
    in_specs = [row(d), row(widths[0]), row(widths[1]), row(widths[2]), _full(w_out.shape), mod_spec,
                _full((1, d)), mod_spec, mod_spec, mod_spec, _full(w_gu.shape), _full(w_down.shape)]
    args = [xs, lru, diff, na, w_out, gate1, n2g.reshape(1, d), shift, scale, gate2, w_gu, w_down]
    final = final_g is not None
    if final:
        in_specs.append(_full((1, d)))
        args.append(final_g.reshape(1, d))
    return pl.pallas_call(
        functools.partial(_out_ffn_kernel, widths=widths, hidden=hidden, h_chunk=h_chunk, final=final),
        grid=(t // tm,),
        in_specs=in_specs,
        out_specs=row(d),
        out_shape=jax.ShapeDtypeStruct((t, d), F32),
        compiler_params=_cparams(("arbitrary",)),
        name="out_ffn_final" if final else "out_ffn",
    )(*args)


def _rope_tables(seq):
    tpos = jnp.arange(seq, dtype=jnp.int32)
    row = (tpos // GRID_W).astype(F32)
    col = (tpos % GRID_W).astype(F32)
    axis_dim = HEAD_DIM // 2
    inv_freq = 1.0 / (ROPE_BASE ** (jnp.arange(0, axis_dim, 2, dtype=F32) / axis_dim))
    ang_r = row[:, None] * inv_freq[None]
    ang_c = col[:, None] * inv_freq[None]
    cos = jnp.concatenate([jnp.cos(ang_r)] * 2 + [jnp.cos(ang_c)] * 2, axis=-1)
    sin = jnp.concatenate([-jnp.sin(ang_r), jnp.sin(ang_r), -jnp.sin(ang_c), jnp.sin(ang_c)], axis=-1)
    return jnp.concatenate([cos, cos], axis=-1), jnp.concatenate([sin, sin], axis=-1)


def _block_diag(w):
    nb, n, _ = w.shape
    eye = jnp.eye(nb, dtype=w.dtype)
    return (eye[:, None, :, None] * w[:, :, None, :]).reshape(nb * n, nb * n)


def _pick_tile(n, target):
    t = min(n, target)
    while n % t:
        t //= 2
    return t


def kernel(x, c, ctx, c_ctx, w_mod, b_mod, norm1_g, norm2_g, w_in, lru_conv_w, lru_conv_b, lru_wa, lru_ba,
           lru_wx, lru_bx, lru_lambda, diff_lambda, diff_subln_g, na_rpb, w_out, ffn_w_gu, ffn_w_down, final_g):
    batch, seq, d = x.shape
    n_ctx = ctx.shape[1]
    depth = w_mod.shape[0]
    lru_w = lru_conv_w.shape[-1]
    diff_w = d // 2
    na_w = d - lru_w - diff_w
    sections = (lru_w, lru_w, diff_w, diff_w, diff_w, na_w, na_w, na_w)
    assert sum(sections) == w_in.shape[-1], (sections, w_in.shape)
    hidden = ffn_w_down.shape[1]

    cvec = jnp.concatenate([c, c_ctx[None], jnp.zeros((8 - batch - 1, d), F32)], axis=0)
    mods = _mod_call(cvec, w_mod, b_mod)
    rope_tabs = _rope_tables(seq)

    xs = x.reshape(batch * seq, d)
    hs = ctx.reshape(batch * n_ctx, d)
    tm = _pick_tile(seq, 512)
    tm_c = _pick_tile(n_ctx, 256)
    tc = _pick_tile(seq, 256)
    tc_c = _pick_tile(n_ctx, 256)
    zeros_h = jnp.zeros((batch, lru_w), F32)

    for l in range(depth):
        last = l == depth - 1
        lam_init = 0.8 - 0.6 * math.exp(-0.3 * l)
        mx = [mods[l, :batch, i * d:(i + 1) * d].reshape(batch, 1, d) for i in range(6)]
        mc = [mods[l, batch:batch + 1, i * d:(i + 1) * d].reshape(1, 1, d) for i in range(6)]
        w_in_b = w_in[l].astype(BF16)
        w_out_b = w_out[l].astype(BF16)
        w_gu_b = ffn_w_gu[l].astype(BF16)
        w_down_b = ffn_w_down[l].astype(BF16)
        w_gates = [jnp.concatenate([_block_diag(lru_wa[l, dr]), _block_diag(lru_wx[l, dr])], axis=1).astype(BF16)
                   for dr in range(2)]
        b_gates = [jnp.concatenate([lru_ba[l, dr], lru_bx[l, dr]]) for dr in range(2)]

        px = _in_proj_call(xs, norm1_g[l], mx[0], mx[1], w_in_b, rope_tabs,
                           sections=sections, rows_per_batch=seq, tm=tm)
        pc = _in_proj_call(hs, norm1_g[l], mc[0], mc[1], w_in_b, None,
                           sections=sections, rows_per_batch=n_ctx, tm=tm_c)
        xl, gl, qd, kd, vd, qn, kn, vn = px
        cxl, cgl, cqd, ckd, cvd, cqn, ckn, cvn = pc

        lru_args = (lru_conv_w[l], lru_conv_b[l])
        hc_f = _lru_call(cxl, *lru_args, w_gates[0], b_gates[0], lru_lambda[l, 0], zeros_h,
                         seq=n_ctx, reverse=False, tc=tc_c)
        hc_b, lru_c = _lru_call(cxl, *lru_args, w_gates[1], b_gates[1], lru_lambda[l, 1], zeros_h, hc_f, cgl,
                                seq=n_ctx, reverse=True, tc=tc_c)
        hcf_last = hc_f.reshape(batch, n_ctx, lru_w)[:, -1]
        hcb_first = hc_b.reshape(batch, n_ctx, lru_w)[:, 0]
        hl_f = _lru_call(xl, *lru_args, w_gates[0], b_gates[0], lru_lambda[l, 0], hcf_last,
                         seq=seq, reverse=False, tc=tc)
        _, lru_x = _lru_call(xl, *lru_args, w_gates[1], b_gates[1], lru_lambda[l, 1], hcb_first, hl_f, gl,
                             seq=seq, reverse=True, tc=tc)

        keys = jnp.concatenate([ckd.reshape(batch, n_ctx, diff_w), kd.reshape(batch, seq, diff_w)], axis=1)
        vals = jnp.concatenate([cvd.reshape(batch, n_ctx, diff_w), vd.reshape(batch, seq, diff_w)], axis=1)
        n_keys = n_ctx + seq
        tk = n_ctx
        for cand in (768, 512, 384, 256):
            if n_keys % cand == 0:
                tk = cand
                break
        diff_x = _diff_attn_call(qd, keys, vals, diff_lambda[l], diff_subln_g[l],
                                 seq=seq, n_keys=n_keys, tq=_pick_tile(seq, 256), tk=tk, lam_init=lam_init)

        bias = _rpb_table(na_rpb[l])
        ckn3 = ckn.reshape(batch, n_ctx, na_w)
        cvn3 = cvn.reshape(batch, n_ctx, na_w)
        na_x = _na_call(qn, kn.reshape(batch, seq, na_w), vn.reshape(batch, seq, na_w), ckn3, cvn3, bias, seq=seq)

        xs = _out_ffn_call(xs, lru_x, diff_x, na_x, w_out_b, mx[2], norm2_g[l], mx[3], mx[4], mx[5],
                           w_gu_b, w_down_b, final_g if last else None,
                           rows_per_batch=seq, tm=tm, h_chunk=hidden // 2)
        if not last:
            diff_c = _diff_attn_call(cqd, ckd.reshape(batch, n_ctx, diff_w), cvd.reshape(batch, n_ctx, diff_w),
                                     diff_lambda[l], diff_subln_g[l], seq=n_ctx, n_keys=n_ctx,
                                     tq=n_ctx, tk=n_ctx, lam_init=lam_init)
            na_c = _plain_attn_call(cqn, ckn, cvn, seq=n_ctx)
            hs = _out_ffn_call(hs, lru_c, diff_c, na_c, w_out_b, mc[2], norm2_g[l], mc[3], mc[4], mc[5],
                               w_gu_b, w_down_b, None, rows_per_batch=n_ctx, tm=tm_c, h_chunk=hidden // 2)
    return xs.reshape(batch, seq, d)
```

```python
import functools
import math

import jax
import jax.numpy as jnp
from jax import lax
from jax.experimental import pallas as pl
from jax.experimental.pallas import tpu as pltpu

GRID_W = 64
HEAD_DIM = 64
LRU_BLOCKS = 4
LRU_CONV = 4
LRU_C = 8.0
NA_KH = 8
NA_KW = 16
ROPE_BASE = 10000.0
NORM_EPS = 1e-6
NEG_INF = -1e30
LOG2E = math.log2(math.e)
MXU_K = 256
CONV_HALO = 8
VMEM_LIMIT = 56 * 1024 * 1024

F32 = jnp.float32
BF16 = jnp.bfloat16


def _cparams(sem):
    return pltpu.CompilerParams(dimension_semantics=sem, vmem_limit_bytes=VMEM_LIMIT)


def _full(shape):
    nd = len(shape)
    return pl.BlockSpec(shape, lambda *_: (0,) * nd, pipeline_mode=pl.Buffered(1))


def _rms(x, g):
    return x * lax.rsqrt(jnp.mean(x * x, axis=-1, keepdims=True) + NORM_EPS) * g


def _sigmoid(x):
    return 1.0 / (1.0 + jnp.exp(-x))


def _dot_t(a, b):
    return lax.dot_general(a, b, (((1,), (1,)), ((), ())), preferred_element_type=F32)


def _mod_kernel(c_ref, w_ref, b_ref, o_ref):
    cv = c_ref[...]
    cond = cv * _sigmoid(cv)
    o_ref[...] = jnp.dot(cond, w_ref[...], preferred_element_type=F32,
                         precision=lax.Precision.HIGHEST) + b_ref[...]


def _mod_call(cvec, w_mod, b_mod):
    depth, d, n = w_mod.shape
    rows = cvec.shape[0]
    tn = d
    return pl.pallas_call(
        _mod_kernel,
        grid=(depth, n // tn),
        in_specs=[pl.BlockSpec((rows, d), lambda l, j: (0, 0)),
                  pl.BlockSpec((None, d, tn), lambda l, j: (l, 0, j)),
                  pl.BlockSpec((None, 1, tn), lambda l, j: (l, 0, j))],
        out_specs=pl.BlockSpec((None, rows, tn), lambda l, j: (l, 0, j)),
        out_shape=jax.ShapeDtypeStruct((depth, rows, n), F32),
        compiler_params=_cparams(("arbitrary", "arbitrary")),
        name="mod",
    )(cvec, w_mod, b_mod.reshape(depth, 1, n))


def _rope(p, cos, sin_signed):
    n = p.shape[-1]
    lane = lax.broadcasted_iota(jnp.int32, p.shape, 1)
    low = (lane % 32) < 16
    partner = jnp.where(low, pltpu.roll(p, n - 16, 1), pltpu.roll(p, 16, 1))
    return p * cos + partner * sin_signed


def _in_proj_kernel(*refs, sections, rope, q_scale):
    if rope:
        x_ref, g_ref, sh_ref, sc_ref, w_ref, cos_ref, sin_ref = refs[:7]
        outs = refs[7:]
    else:
        x_ref, g_ref, sh_ref, sc_ref, w_ref = refs[:5]
        outs = refs[5:]
    h = _rms(x_ref[...], g_ref[...]) * (1.0 + sc_ref[...]) + sh_ref[...]
    hb = h.astype(BF16)
    off = 0
    for idx, (width, o_ref) in enumerate(zip(sections, outs)):
        p = jnp.dot(hb, w_ref[:, off:off + width], preferred_element_type=F32)
        if rope and idx in (2, 3):
            reps = width // cos_ref.shape[-1]
            cos = jnp.concatenate([cos_ref[...]] * reps, axis=-1)
            sin = jnp.concatenate([sin_ref[...]] * reps, axis=-1)
            p = _rope(p, cos, sin)
        if idx in (2, 5):
            p = p * q_scale
        o_ref[...] = p.astype(o_ref.dtype)
        off += width


def _in_proj_call(xs, g, shift, scale, w, rope_tabs, *, sections, rows_per_batch, tm):
    t, d = xs.shape
    n_mod = shift.shape[0]
    per_batch = rows_per_batch // tm
    mod_idx = (lambda i: (i // per_batch, 0, 0)) if n_mod > 1 else (lambda i: (0, 0, 0))
    in_specs = [pl.BlockSpec((tm, d), lambda i: (i, 0)),
                _full((1, d)),
                pl.BlockSpec((None, 1, d), mod_idx),
                pl.BlockSpec((None, 1, d), mod_idx),
                _full(w.shape)]
    args = [xs, g.reshape(1, d), shift, scale, w]
    rope = rope_tabs is not None
    if rope:
        cos, sin = rope_tabs
        tw = cos.shape[-1]
        in_specs += [pl.BlockSpec((tm, tw), lambda i: (i % per_batch, 0))] * 2
        args += [cos, sin]
    dtypes = (F32, F32, BF16, BF16, BF16, BF16, BF16, BF16)
    out_shape = tuple(jax.ShapeDtypeStruct((t, wd), dt) for wd, dt in zip(sections, dtypes))
    out_specs = tuple(pl.BlockSpec((tm, wd), lambda i: (i, 0)) for wd in sections)
    return pl.pallas_call(
        functools.partial(_in_proj_kernel, sections=sections, rope=rope, q_scale=HEAD_DIM ** -0.5 * LOG2E),
        grid=(t // tm,),
        in_specs=in_specs,
        out_specs=out_specs,
        out_shape=out_shape,
        compiler_params=_cparams(("arbitrary",)),
        name="in_proj_rope" if rope else "in_proj",
    )(*args)


def _lru_kernel(*refs, reverse, combine, tc, n_chunks):
    if combine:
        (prev_ref, x_ref, next_ref, cw_ref, cb_ref, w_ref, bias_ref, lam_ref, h0_ref,
         hf_ref, gate_ref, h_out_ref, y_out_ref, carry_ref) = refs
    else:
        (prev_ref, x_ref, next_ref, cw_ref, cb_ref, w_ref, bias_ref, lam_ref, h0_ref,
         h_out_ref, carry_ref) = refs
    j = pl.program_id(1)
    chunk = (n_chunks - 1 - j) if reverse else j

    @pl.when(j == 0)
    def _():
        carry_ref[...] = h0_ref[...]

    width = x_ref.shape[-1]
    has_prev = (chunk > 0).astype(F32)
    has_next = (chunk < n_chunks - 1).astype(F32)
    xcat = jnp.concatenate([prev_ref[...] * has_prev, x_ref[...], next_ref[...] * has_next], axis=0)
    left = LRU_CONV // 2
    u = cb_ref[...]
    for tap in range(LRU_CONV):
        start = CONV_HALO - left + tap
        u = u + cw_ref[tap:tap + 1, :] * xcat[start:start + tc, :]

    gates = jnp.dot(u.astype(BF16), w_ref[...], preferred_element_type=F32) + bias_ref[...]
    r = _sigmoid(gates[:, :width])
    i = _sigmoid(gates[:, width:])
    lam = lam_ref[...]
    log_sig = jnp.minimum(lam, 0.0) - jnp.log(1.0 + jnp.exp(-jnp.abs(lam)))
    log_a = LRU_C * r * log_sig
    a = jnp.exp(log_a)
    b = jnp.sqrt(1.0 - jnp.exp(2.0 * log_a)) * (i * u)

    t_idx = lax.broadcasted_iota(jnp.int32, (tc, width), 0)
    d = 1
    while d < tc:
        if reverse:
            valid = t_idx < tc - d
            shift = tc - d
        else:
            valid = t_idx >= d
            shift = d
        a_sh = jnp.where(valid, pltpu.roll(a, shift, 0), 1.0)
        b_sh = jnp.where(valid, pltpu.roll(b, shift, 0), 0.0)
        b = a * b_sh + b
        a = a * a_sh
        d *= 2
    h = b + a * carry_ref[...]
    carry_ref[...] = h[0:1, :] if reverse else h[tc - 1:tc, :]
    h_out_ref[...] = h
    if combine:
        g = gate_ref[...]
        gelu = 0.5 * g * (1.0 + jnp.tanh(math.sqrt(2.0 / math.pi) * (g + 0.044715 * (g * g * g))))
        y_out_ref[...] = ((hf_ref[...] + h) * gelu).astype(y_out_ref.dtype)


def _lru_call(xl, conv_w, conv_b, w_gates, b_gates, lam, h0, hf=None, gate=None, *,
              seq, reverse, tc):
    t, width = xl.shape
    batch = t // seq
    n_chunks = seq // tc
    hb = tc // CONV_HALO
    n_halo = seq // CONV_HALO
    combine = hf is not None

    def chunk_of(j):
        return (n_chunks - 1 - j) if reverse else j

    def main_idx(b, j):
        return (b * n_chunks + chunk_of(j), 0)

    def prev_idx(b, j):
        return (b * n_halo + jnp.maximum(chunk_of(j) * hb - 1, 0), 0)

    def next_idx(b, j):
        return (b * n_halo + jnp.minimum((chunk_of(j) + 1) * hb, n_halo - 1), 0)

    in_specs = [pl.BlockSpec((CONV_HALO, width), prev_idx),
                pl.BlockSpec((tc, width), main_idx),
                pl.BlockSpec((CONV_HALO, width), next_idx),
                _full(conv_w.shape), _full((1, width)), _full(w_gates.shape),
                _full((1, 2 * width)), _full((1, width)),
                pl.BlockSpec((None, 1, width), lambda b, j: (b, 0, 0))]
    args = [xl, xl, xl, conv_w, conv_b.reshape(1, width), w_gates, b_gates.reshape(1, 2 * width),
            lam.reshape(1, width), h0.reshape(batch, 1, width)]
    out_shape = [jax.ShapeDtypeStruct((t, width), F32)]
    out_specs = [pl.BlockSpec((tc, width), main_idx)]
    if combine:
        in_specs += [pl.BlockSpec((tc, width), main_idx)] * 2
        args += [hf, gate]
        out_shape.append(jax.ShapeDtypeStruct((t, width), BF16))
        out_specs.append(pl.BlockSpec((tc, width), main_idx))
    res = pl.pallas_call(
        functools.partial(_lru_kernel, reverse=reverse, combine=combine, tc=tc, n_chunks=n_chunks),
        grid=(batch, n_chunks),
        in_specs=in_specs,
        out_specs=tuple(out_specs),
        out_shape=tuple(out_shape),
        scratch_shapes=[pltpu.VMEM((1, width), F32)],
        compiler_params=_cparams(("arbitrary", "arbitrary")),
        name="lru_bwd" if reverse else "lru_fwd",
    )(*args)
    return res if combine else res[0]


def _diff_lambda(dl_ref, lam_init):
    dl = dl_ref[...]
    s01 = jnp.sum(dl[0:1, :] * dl[1:2, :], axis=-1, keepdims=True)
    s23 = jnp.sum(dl[2:3, :] * dl[3:4, :], axis=-1, keepdims=True)
    return jnp.exp(s01) - jnp.exp(s23) + lam_init


def _diff_attn_kernel(*refs, tq, tk, n_kt, lam_init):
    if n_kt:
        q_ref, kc_ref, vc_ref, k_ref, v_ref, dl_ref, g_ref, o_ref, s_a, s_b = refs
    else:
        q_ref, kc_ref, vc_ref, dl_ref, g_ref, o_ref = refs
    q = q_ref[...]
    lane = lax.broadcasted_iota(jnp.int32, q.shape, 1)
    zero = jnp.zeros_like(q)
    qs = jnp.concatenate([jnp.where(lane < HEAD_DIM, q, zero), jnp.where(lane >= HEAD_DIM, q, zero)], axis=0)
    vw = vc_ref.shape[-1]

    def scores(kt):
        s = _dot_t(qs, kt)
        return s, jnp.max(s, axis=-1, keepdims=True)

    def update(load_s, vt, m_tile, m, acc):
        m_new = jnp.maximum(m, m_tile)
        acc = jnp.exp2(m - m_new) * acc
        v_ext = jnp.concatenate([vt, jnp.ones_like(vt)], axis=1)
        n_keys = vt.shape[0]
        kc = min(n_keys, MXU_K)
        for c0 in range(0, n_keys, kc):
            p = jnp.exp2(load_s(c0, kc) - m_new).astype(BF16)
            acc = acc + jnp.dot(p, v_ext[c0:c0 + kc, :], preferred_element_type=F32)
        return m_new, acc

    def k_tile(j):
        return k_ref[pl.ds(pl.multiple_of(j * tk, tk), tk), :]

    def v_tile(j):
        return v_ref[pl.ds(pl.multiple_of(j * tk, tk), tk), :]

    m = jnp.full((2 * tq, 1), -jnp.inf, F32)
    acc = jnp.zeros((2 * tq, 2 * vw), F32)
    s_c, mt_c = scores(kc_ref[...])
    if n_kt:
        s_first, mt_a = scores(k_tile(0))
        s_a[...] = s_first
    m, acc = update(lambda c0, n: s_c[:, c0:c0 + n], vc_ref[...], mt_c, m, acc)
    if n_kt:
        def pair(j, carry, last):
            m, acc, mt_a = carry
            s_next, mt_b = scores(k_tile(j + 1))
            s_b[...] = s_next
            m, acc = update(lambda c0, n: s_a[:, c0:c0 + n], v_tile(j), mt_a, m, acc)
            if not last:
                s_next, mt_a = scores(k_tile(j + 2))
                s_a[...] = s_next
            m, acc = update(lambda c0, n: s_b[:, c0:c0 + n], v_tile(j + 1), mt_b, m, acc)
            return m, acc, mt_a

        carry = lax.fori_loop(0, n_kt // 2 - 1, lambda t, c: pair(2 * t, c, False), (m, acc, mt_a))
        m, acc, _ = pair(n_kt - 2, carry, True)
    o = acc[:, :vw] / acc[:, vw:vw + 1]
    lam = _diff_lambda(dl_ref, lam_init)
    od = o[:tq, :] - lam * o[tq:, :]
    o_ref[...] = (_rms(od, g_ref[...]) * (1.0 - lam_init)).astype(o_ref.dtype)


def _diff_attn_call(q, kc, vc, k, v, dl, g, *, seq, tq, tk, lam_init):
    t, width = q.shape
    batch = t // seq
    hw = 2 * HEAD_DIM
    heads = width // hw
    nq = seq // tq
    n_ctx = kc.shape[1]
    kv_spec = lambda n: pl.BlockSpec((None, n, hw), lambda b, h, i: (b, 0, h))
    in_specs = [pl.BlockSpec((tq, hw), lambda b, h, i: (b * nq + i, h)), kv_spec(n_ctx), kv_spec(n_ctx)]
    args = [q, kc, vc]
    n_kt = 0
    if k is not None:
        n_lat = k.shape[1]
        n_kt = n_lat // tk
        in_specs += [kv_spec(n_lat), kv_spec(n_lat)]
        args += [k, v]
    in_specs += [_full(dl.shape), _full((1, hw))]
    args += [dl, g.reshape(1, hw)]
    assert n_kt % 2 == 0, n_kt
    scratch = [pltpu.VMEM((2 * tq, tk), F32)] * 2 if n_kt else []
    return pl.pallas_call(
        functools.partial(_diff_attn_kernel, tq=tq, tk=tk, n_kt=n_kt, lam_init=lam_init),
        grid=(batch, heads, nq),
        in_specs=in_specs,
        out_specs=pl.BlockSpec((tq, hw), lambda b, h, i: (b * nq + i, h)),
        out_shape=jax.ShapeDtypeStruct((t, width), BF16),
        scratch_shapes=scratch,
        compiler_params=_cparams(("arbitrary", "arbitrary", "arbitrary")),
        name="diff_attn" if n_kt else "diff_attn_ctx",
    )(*args)


def _rpb_table_kernel(r_ref, o_ref):
    n = o_ref.shape[-1]
    j = lax.broadcasted_iota(jnp.int32, (r_ref.shape[-1], n), 0)
    pos = lax.broadcasted_iota(jnp.int32, (r_ref.shape[-1], n), 1)
    qc = pos // GRID_W
    kc = pos % GRID_W
    coff = jnp.clip(kc - qc + (NA_KW - 1), 0, 2 * NA_KW - 2)
    onehot = (coff == j).astype(F32)
    vals = jnp.dot(r_ref[...], onehot, preferred_element_type=F32, precision=lax.Precision.HIGHEST)
    pos1 = lax.broadcasted_iota(jnp.int32, vals.shape, 1)
    qc1 = pos1 // GRID_W
    kc1 = pos1 % GRID_W
    c0 = jnp.clip(qc1 - NA_KW // 2, 0, GRID_W - NA_KW)
    in_win = (kc1 >= c0) & (kc1 < c0 + NA_KW)
    o_ref[...] = jnp.where(in_win, vals * LOG2E, NEG_INF)


def _rpb_table(rpb):
    heads = rpb.shape[0]
    ncol = 2 * NA_KW
    rp = jnp.pad(rpb, ((0, 0), (0, 0), (0, ncol - rpb.shape[-1])))
    r_all = jnp.stack([rp[:, NA_KH - 1 - p:2 * NA_KH - 1 - p, :] for p in range(NA_KH)], axis=0)
    r_all = r_all.reshape(NA_KH, heads * NA_KH, ncol)
    n = GRID_W * GRID_W
    tab = pl.pallas_call(
        _rpb_table_kernel,
        grid=(NA_KH,),
        in_specs=[pl.BlockSpec((None, heads * NA_KH, ncol), lambda p: (p, 0, 0))],
        out_specs=pl.BlockSpec((None, heads * NA_KH, n), lambda p: (p, 0, 0)),
        out_shape=jax.ShapeDtypeStruct((NA_KH, heads * NA_KH, n), F32),
        compiler_params=_cparams(("arbitrary",)),
        name="rpb_table",
    )(r_all)
    tab = tab.reshape(NA_KH, heads, NA_KH, GRID_W, GRID_W).transpose(0, 1, 3, 2, 4)
    return tab.reshape(NA_KH, heads * GRID_W, NA_KH * GRID_W)


def _stack_heads(q, heads):
    head_of_lane = lax.broadcasted_iota(jnp.int32, q.shape, 1) // HEAD_DIM
    zero = jnp.zeros_like(q)
    return jnp.concatenate([jnp.where(head_of_lane == h, q, zero) for h in range(heads)], axis=0)


def _unstack_heads(res, heads):
    n = res.shape[0] // heads
    head_of_lane = lax.broadcasted_iota(jnp.int32, (n, res.shape[1]), 1) // HEAD_DIM
    out = jnp.zeros((n, res.shape[1]), res.dtype)
    for h in range(heads):
        out = jnp.where(head_of_lane == h, res[h * n:(h + 1) * n, :], out)
    return out


def _na_kernel(q_ref, k_ref, v_ref, kc_ref, vc_ref, bias_ref, o_ref, *, rows, heads, row_block):
    n_loc = NA_KH * GRID_W
    for i in range(row_block):
        r = pl.program_id(1) * row_block + i
        r0 = jnp.clip(r - NA_KH // 2, 0, rows - NA_KH)
        pat = r - r0
        start = pl.multiple_of(r0 * GRID_W, GRID_W)
        ks = k_ref[pl.ds(start, n_loc), :]
        vs = v_ref[pl.ds(start, n_loc), :]
        qs = _stack_heads(q_ref[i * GRID_W:(i + 1) * GRID_W, :], heads)
        s_loc = _dot_t(qs, ks) + bias_ref[pat]
        s_ctx = _dot_t(qs, kc_ref[...])
        m = jnp.maximum(jnp.max(s_loc, axis=-1, keepdims=True), jnp.max(s_ctx, axis=-1, keepdims=True))
        e_loc = jnp.exp2(s_loc - m)
        e_ctx = jnp.exp2(s_ctx - m)
        denom = jnp.sum(e_loc, axis=-1, keepdims=True) + jnp.sum(e_ctx, axis=-1, keepdims=True)
        res = (jnp.dot(e_loc.astype(BF16), vs, preferred_element_type=F32)
               + jnp.dot(e_ctx.astype(BF16), vc_ref[...], preferred_element_type=F32)) / denom
        o_ref[i * GRID_W:(i + 1) * GRID_W, :] = _unstack_heads(res, heads).astype(o_ref.dtype)


def _na_call(q, k, v, kc, vc, bias, *, seq, row_block):
    t, width = q.shape
    batch = t // seq
    rows = seq // GRID_W
    assert rows >= NA_KH and rows % row_block == 0, (rows, row_block)
    heads = width // HEAD_DIM
    n_ctx = kc.shape[1]
    steps = rows // row_block
    tq = row_block * GRID_W
    return pl.pallas_call(
        functools.partial(_na_kernel, rows=rows, heads=heads, row_block=row_block),
        grid=(batch, steps),
        in_specs=[pl.BlockSpec((tq, width), lambda b, r: (b * steps + r, 0)),
                  pl.BlockSpec((None, seq, width), lambda b, r: (b, 0, 0)),
                  pl.BlockSpec((None, seq, width), lambda b, r: (b, 0, 0)),
                  pl.BlockSpec((None, n_ctx, width), lambda b, r: (b, 0, 0)),
                  pl.BlockSpec((None, n_ctx, width), lambda b, r: (b, 0, 0)),
                  _full(bias.shape)],
        out_specs=pl.BlockSpec((tq, width), lambda b, r: (b * steps + r, 0)),
        out_shape=jax.ShapeDtypeStruct((t, width), BF16),
        compiler_params=_cparams(("arbitrary", "arbitrary")),
        name="na_attn",
    )(q, k, v, kc, vc, bias)


def _plain_attn_kernel(q_ref, k_ref, v_ref, o_ref, *, heads):
    qs = _stack_heads(q_ref[...], heads)
    s = _dot_t(qs, k_ref[...])
    e = jnp.exp2(s - jnp.max(s, axis=-1, keepdims=True))
    res = jnp.dot(e.astype(BF16), v_ref[...], preferred_element_type=F32) / jnp.sum(e, axis=-1, keepdims=True)
    o_ref[...] = _unstack_heads(res, heads).astype(o_ref.dtype)


def _plain_attn_call(q, k, v, *, seq):
    t, width = q.shape
    spec = pl.BlockSpec((seq, width), lambda b: (b, 0))
    return pl.pallas_call(
        functools.partial(_plain_attn_kernel, heads=width // HEAD_DIM),
        grid=(t // seq,),
        in_specs=[spec, spec, spec],
        out_specs=spec,
        out_shape=jax.ShapeDtypeStruct((t, width), BF16),
        compiler_params=_cparams(("arbitrary",)),
        name="ctx_plain_attn",
    )(q, k, v)


def _out_ffn_kernel(*refs, widths, hidden, h_chunk, final):
    (x_ref, lru_ref, diff_ref, na_ref, wo_ref, g1_ref, n2_ref, sh_ref, sc_ref, g2_ref,
     wgu_ref, wd_ref) = refs[:12]
    if final:
        fg_ref, o_ref = refs[12:]
    else:
        o_ref = refs[12]
    off = 0
    mix = None
    for part_ref, wd_ in zip((lru_ref, diff_ref, na_ref), widths):
        term = jnp.dot(part_ref[...], wo_ref[off:off + wd_, :], preferred_element_type=F32)
        mix = term if mix is None else mix + term
        off += wd_
    x1 = x_ref[...] + g1_ref[...] * mix
    hb = (_rms(x1, n2_ref[...]) * (1.0 + sc_ref[...]) + sh_ref[...]).astype(BF16)
    acc = None
    for c0 in range(0, hidden, h_chunk):
        g = jnp.dot(hb, wgu_ref[:, c0:c0 + h_chunk], preferred_element_type=F32)
        u = jnp.dot(hb, wgu_ref[:, hidden + c0:hidden + c0 + h_chunk], preferred_element_type=F32)
        a = (g * _sigmoid(g) * u).astype(BF16)
        term = jnp.dot(a, wd_ref[c0:c0 + h_chunk, :], preferred_element_type=F32)
        acc = term if acc is None else acc + term
    x2 = x1 + g2_ref[...] * acc
    if final:
        x2 = _rms(x2, fg_ref[...])
    o_ref[...] = x2


def _out_ffn_call(xs, lru, diff, na, w_out, gate1, n2g, shift, scale, gate2, w_gu, w_down, final_g,
                  *, rows_per_batch, tm, h_chunk):
    t, d = xs.shape
    hidden = w_down.shape[0]
    n_mod = shift.shape[0]
    per_batch = rows_per_batch // tm
    mod_idx = (lambda i: (i // per_batch, 0, 0)) if n_mod > 1 else (lambda i: (0, 0, 0))
    mod_spec = pl.BlockSpec((None, 1, d), mod_idx)
    row = lambda wd_: pl.BlockSpec((tm, wd_), lambda i: (i, 0))
    widths = (lru.shape[1], diff.shape[1], na.shape[1])
    in_specs = [row(d), row(widths[0]), row(widths[1]), row(widths[2]), _full(w_out.shape), mod_spec,
                _full((1, d)), mod_spec, mod_spec, mod_spec, _full(w_gu.shape), _full(w_down.shape)]
    args = [xs, lru, diff, na, w_out, gate1, n2g.reshape(1, d), shift, scale, gate2, w_gu, w_down]
    final = final_g is not None
    if final:
        in_specs.append(_full((1, d)))
        args.append(final_g.reshape(1, d))
    return pl.pallas_call(
        functools.partial(_out_ffn_kernel, widths=widths, hidden=hidden, h_chunk=h_chunk, final=final),
        grid=(t // tm,),
        in_specs=in_specs,
        out_specs=row(d),
        out_shape=jax.ShapeDtypeStruct((t, d), F32),
        compiler_params=_cparams(("arbitrary",)),
        name="out_ffn_final" if final else "out_ffn",
    )(*args)


def _rope_tables(seq):
    tpos = jnp.arange(seq, dtype=jnp.int32)
    row = (tpos // GRID_W).astype(F32)
    col = (tpos % GRID_W).astype(F32)
    axis_dim = HEAD_DIM // 2
    inv_freq = 1.0 / (ROPE_BASE ** (jnp.arange(0, axis_dim, 2, dtype=F32) / axis_dim))
    ang_r = row[:, None] * inv_freq[None]
    ang_c = col[:, None] * inv_freq[None]
    cos = jnp.concatenate([jnp.cos(ang_r)] * 2 + [jnp.cos(ang_c)] * 2, axis=-1)
    sin = jnp.concatenate([-jnp.sin(ang_r), jnp.sin(ang_r), -jnp.sin(ang_c), jnp.sin(ang_c)], axis=-1)
    return jnp.concatenate([cos, cos], axis=-1), jnp.concatenate([sin, sin], axis=-1)


def _block_diag(w):
    nb, n, _ = w.shape
    eye = jnp.eye(nb, dtype=w.dtype)
    return (eye[:, None, :, None] * w[:, :, None, :]).reshape(nb * n, nb * n)


def _pick_tile(n, target):
    t = min(n, target)
    while n % t:
        t //= 2
    return t


def kernel(x, c, ctx, c_ctx, w_mod, b_mod, norm1_g, norm2_g, w_in, lru_conv_w, lru_conv_b, lru_wa, lru_ba,
           lru_wx, lru_bx, lru_lambda, diff_lambda, diff_subln_g, na_rpb, w_out, ffn_w_gu, ffn_w_down, final_g):
    batch, seq, d = x.shape
    n_ctx = ctx.shape[1]
    depth = w_mod.shape[0]
    lru_w = lru_conv_w.shape[-1]
    diff_w = d // 2
    na_w = d - lru_w - diff_w
    sections = (lru_w, lru_w, diff_w, diff_w, diff_w, na_w, na_w, na_w)
    assert sum(sections) == w_in.shape[-1], (sections, w_in.shape)
    hidden = ffn_w_down.shape[1]

    cvec = jnp.concatenate([c, c_ctx[None], jnp.zeros((8 - batch - 1, d), F32)], axis=0)
    mods = _mod_call(cvec, w_mod, b_mod)
    rope_tabs = _rope_tables(seq)

    xs = x.reshape(batch * seq, d)
    hs = ctx.reshape(batch * n_ctx, d)
    tm = _pick_tile(seq, 512)
    tm_c = _pick_tile(n_ctx, 256)
    tc = _pick_tile(seq, 256)
    tc_c = _pick_tile(n_ctx, 256)
    zeros_h = jnp.zeros((batch, lru_w), F32)

    for l in range(depth):
        last = l == depth - 1
        lam_init = 0.8 - 0.6 * math.exp(-0.3 * l)
        mx = [mods[l, :batch, i * d:(i + 1) * d].reshape(batch, 1, d) for i in range(6)]
        mc = [mods[l, batch:batch + 1, i * d:(i + 1) * d].reshape(1, 1, d) for i in range(6)]
        w_in_b = w_in[l].astype(BF16)
        w_out_b = w_out[l].astype(BF16)
        w_gu_b = ffn_w_gu[l].astype(BF16)
        w_down_b = ffn_w_down[l].astype(BF16)
        w_gates = [jnp.concatenate([_block_diag(lru_wa[l, dr]), _block_diag(lru_wx[l, dr])], axis=1).astype(BF16)
                   for dr in range(2)]
        b_gates = [jnp.concatenate([lru_ba[l, dr], lru_bx[l, dr]]) for dr in range(2)]

        px = _in_proj_call(xs, norm1_g[l], mx[0], mx[1], w_in_b, rope_tabs,
                           sections=sections, rows_per_batch=seq, tm=tm)
        pc = _in_proj_call(hs, norm1_g[l], mc[0], mc[1], w_in_b, None,
                           sections=sections, rows_per_batch=n_ctx, tm=tm_c)
        xl, gl, qd, kd, vd, qn, kn, vn = px
        cxl, cgl, cqd, ckd, cvd, cqn, ckn, cvn = pc

        lru_args = (lru_conv_w[l], lru_conv_b[l])
        hc_f = _lru_call(cxl, *lru_args, w_gates[0], b_gates[0], lru_lambda[l, 0], zeros_h,
                         seq=n_ctx, reverse=False, tc=tc_c)
        hc_b, lru_c = _lru_call(cxl, *lru_args, w_gates[1], b_gates[1], lru_lambda[l, 1], zeros_h, hc_f, cgl,
                                seq=n_ctx, reverse=True, tc=tc_c)
        hcf_last = hc_f.reshape(batch, n_ctx, lru_w)[:, -1]
        hcb_first = hc_b.reshape(batch, n_ctx, lru_w)[:, 0]
        hl_f = _lru_call(xl, *lru_args, w_gates[0], b_gates[0], lru_lambda[l, 0], hcf_last,
                         seq=seq, reverse=False, tc=tc)
        _, lru_x = _lru_call(xl, *lru_args, w_gates[1], b_gates[1], lru_lambda[l, 1], hcb_first, hl_f, gl,
                             seq=seq, reverse=True, tc=tc)

        ckd3 = ckd.reshape(batch, n_ctx, diff_w)
        cvd3 = cvd.reshape(batch, n_ctx, diff_w)
        diff_x = _diff_attn_call(qd, ckd3, cvd3, kd.reshape(batch, seq, diff_w), vd.reshape(batch, seq, diff_w),
                                 diff_lambda[l], diff_subln_g[l], seq=seq, tq=_pick_tile(seq, 256),
                                 tk=_pick_tile(seq // 4, 2048), lam_init=lam_init)

        bias = _rpb_table(na_rpb[l])
        ckn3 = ckn.reshape(batch, n_ctx, na_w)
        cvn3 = cvn.reshape(batch, n_ctx, na_w)
        na_x = _na_call(qn, kn.reshape(batch, seq, na_w), vn.reshape(batch, seq, na_w), ckn3, cvn3, bias, seq=seq,
                        row_block=8)

        xs = _out_ffn_call(xs, lru_x, diff_x, na_x, w_out_b, mx[2], norm2_g[l], mx[3], mx[4], mx[5],
                           w_gu_b, w_down_b, final_g if last else None,
                           rows_per_batch=seq, tm=tm, h_chunk=hidden // 2)
        if not last:
            diff_c = _diff_attn_call(cqd, ckd3, cvd3, None, None, diff_lambda[l], diff_subln_g[l],
                                     seq=n_ctx, tq=n_ctx, tk=n_ctx, lam_init=lam_init)
            na_c = _plain_attn_call(cqn, ckn, cvn, seq=n_ctx)
            hs = _out_ffn_call(hs, lru_c, diff_c, na_c, w_out_b, mc[2], norm2_g[l], mc[3], mc[4], mc[5],
                               w_gu_b, w_down_b, None, rows_per_batch=n_ctx, tm=tm_c, h_chunk=hidden // 2)
    return xs.reshape(batch, seq, d)
```

```python
import functools
import math

import jax
import jax.numpy as jnp
from jax import lax
from jax.experimental import pallas as pl
from jax.experimental.pallas import tpu as pltpu

GRID_W = 64
HEAD_DIM = 64
LRU_BLOCKS = 4
LRU_CONV = 4
LRU_C = 8.0
NA_KH = 8
NA_KW = 16
ROPE_BASE = 10000.0
NORM_EPS = 1e-6
NEG_INF = -1e30
LOG2E = math.log2(math.e)
MXU_K = 256
MXU_N = 256
VT_ROWS = 2 * HEAD_DIM + 16
SUBLANES = 8
MOD_ROWS = 8
CONV_HALO = 8
VMEM_LIMIT = 56 * 1024 * 1024

F32 = jnp.float32
BF16 = jnp.bfloat16


def _cparams(sem):
    return pltpu.CompilerParams(dimension_semantics=sem, vmem_limit_bytes=VMEM_LIMIT)


def _full(shape):
    nd = len(shape)
    return pl.BlockSpec(shape, lambda *_: (0,) * nd, pipeline_mode=pl.Buffered(1))


def _layer(arr, l):
    tail = arr.shape[1:]
    return pl.BlockSpec((None,) + tail, lambda *_: (l,) + (0,) * len(tail), pipeline_mode=pl.Buffered(1))


def _mod_spec(mods, l, k, per_batch, shared_row):
    d = mods.shape[-1]
    base = (l * MOD_ROWS) * 6 + k
    if shared_row is not None:
        return pl.BlockSpec((None, 1, d), lambda i: (base + shared_row * 6, 0, 0))
    return pl.BlockSpec((None, 1, d), lambda i: (base + (i // per_batch) * 6, 0, 0))


def _rms(x, g):
    return x * lax.rsqrt(jnp.mean(x * x, axis=-1, keepdims=True) + NORM_EPS) * g


def _sigmoid(x):
    return 1.0 / (1.0 + jnp.exp(-x))


def _dot_t(a, b):
    return lax.dot_general(a, b, (((1,), (1,)), ((), ())), preferred_element_type=F32)


def _mod_kernel(c_ref, w_ref, b_ref, o_ref):
    cv = c_ref[...]
    cond = cv * _sigmoid(cv)
    o_ref[...] = jnp.dot(cond, w_ref[...], preferred_element_type=F32,
                         precision=lax.Precision.HIGHEST) + b_ref[...]


def _mod_call(cvec, w_mod, b_mod):
    depth, d, n = w_mod.shape
    rows = cvec.shape[0]
    tn = d
    return pl.pallas_call(
        _mod_kernel,
        grid=(depth, n // tn),
        in_specs=[pl.BlockSpec((rows, d), lambda l, j: (0, 0)),
                  pl.BlockSpec((None, d, tn), lambda l, j: (l, 0, j)),
                  pl.BlockSpec((None, 1, tn), lambda l, j: (l, 0, j))],
        out_specs=pl.BlockSpec((None, rows, tn), lambda l, j: (l, 0, j)),
        out_shape=jax.ShapeDtypeStruct((depth, rows, n), F32),
        compiler_params=_cparams(("arbitrary", "arbitrary")),
        name="mod",
    )(cvec, w_mod, b_mod.reshape(depth, 1, n))


def _rope(p, cos, sin_signed):
    n = p.shape[-1]
    lane = lax.broadcasted_iota(jnp.int32, p.shape, 1)
    low = (lane % 32) < 16
    partner = jnp.where(low, pltpu.roll(p, n - 16, 1), pltpu.roll(p, 16, 1))
    return p * cos + partner * sin_signed


def _in_proj_kernel(*refs, sections, rope, q_scale):
    if rope:
        x_ref, g_ref, sh_ref, sc_ref, w_ref, cos_ref, sin_ref = refs[:7]
        outs = refs[7:]
    else:
        x_ref, g_ref, sh_ref, sc_ref, w_ref = refs[:5]
        outs = refs[5:]
    h = _rms(x_ref[...], g_ref[...]) * (1.0 + sc_ref[...]) + sh_ref[...]
    hb = h.astype(BF16)
    off = 0
    for idx, (width, o_ref) in enumerate(zip(sections, outs)):
        p = jnp.dot(hb, w_ref[:, off:off + width], preferred_element_type=F32)
        if rope and idx in (2, 3):
            reps = width // cos_ref.shape[-1]
            cos = jnp.concatenate([cos_ref[...]] * reps, axis=-1)
            sin = jnp.concatenate([sin_ref[...]] * reps, axis=-1)
            p = _rope(p, cos, sin)
        if idx in (2, 5):
            p = p * q_scale
        o_ref[...] = p.astype(o_ref.dtype)
        off += width


def _in_proj_call(xs, g, mods, w, rope_tabs, *, layer, shared_row, sections, rows_per_batch, tm):
    t, d = xs.shape
    per_batch = rows_per_batch // tm
    in_specs = [pl.BlockSpec((tm, d), lambda i: (i, 0)),
                _layer(g, layer),
                _mod_spec(mods, layer, 0, per_batch, shared_row),
                _mod_spec(mods, layer, 1, per_batch, shared_row),
                _layer(w, layer)]
    args = [xs, g, mods, mods, w]
    rope = rope_tabs is not None
    if rope:
        cos, sin = rope_tabs
        tw = cos.shape[-1]
        in_specs += [pl.BlockSpec((tm, tw), lambda i: (i % per_batch, 0))] * 2
        args += [cos, sin]
    dtypes = (F32, F32, BF16, BF16, BF16, BF16, BF16, BF16)
    out_shape = tuple(jax.ShapeDtypeStruct((t, wd), dt) for wd, dt in zip(sections, dtypes))
    out_specs = tuple(pl.BlockSpec((tm, wd), lambda i: (i, 0)) for wd in sections)
    return pl.pallas_call(
        functools.partial(_in_proj_kernel, sections=sections, rope=rope, q_scale=HEAD_DIM ** -0.5 * LOG2E),
        grid=(t // tm,),
        in_specs=in_specs,
        out_specs=out_specs,
        out_shape=out_shape,
        compiler_params=_cparams(("arbitrary",)),
        name="in_proj_rope" if rope else "in_proj",
    )(*args)


def _lru_kernel(*refs, reverse, combine, tc, n_chunks):
    if combine:
        (prev_ref, x_ref, next_ref, cw_ref, cb_ref, w_ref, bias_ref, lam_ref, h0_ref,
         hf_ref, gate_ref, h_out_ref, y_out_ref, carry_ref) = refs
    else:
        (prev_ref, x_ref, next_ref, cw_ref, cb_ref, w_ref, bias_ref, lam_ref, h0_ref,
         h_out_ref, carry_ref) = refs
    j = pl.program_id(1)
    chunk = (n_chunks - 1 - j) if reverse else j

    @pl.when(j == 0)
    def _():
        carry_ref[...] = h0_ref[...]

    width = x_ref.shape[-1]
    has_prev = (chunk > 0).astype(F32)
    has_next = (chunk < n_chunks - 1).astype(F32)
    xcat = jnp.concatenate([prev_ref[...] * has_prev, x_ref[...], next_ref[...] * has_next], axis=0)
    left = LRU_CONV // 2
    u = cb_ref[...]
    for tap in range(LRU_CONV):
        start = CONV_HALO - left + tap
        u = u + cw_ref[tap:tap + 1, :] * xcat[start:start + tc, :]

    gates = jnp.dot(u.astype(BF16), w_ref[...], preferred_element_type=F32) + bias_ref[...]
    r = _sigmoid(gates[:, :width])
    i = _sigmoid(gates[:, width:])
    lam = lam_ref[...]
    log_sig = jnp.minimum(lam, 0.0) - jnp.log(1.0 + jnp.exp(-jnp.abs(lam)))
    log_a = LRU_C * r * log_sig
    a = jnp.exp(log_a)
    b = jnp.sqrt(1.0 - jnp.exp(2.0 * log_a)) * (i * u)

    n_tiles = tc // SUBLANES
    a = a.reshape(n_tiles, SUBLANES, width)
    b = b.reshape(n_tiles, SUBLANES, width)
    row = lax.broadcasted_iota(jnp.int32, a.shape, 1)
    d = 1
    while d < SUBLANES:
        if reverse:
            valid = row < SUBLANES - d
            shift = SUBLANES - d
        else:
            valid = row >= d
            shift = d
        a_sh = jnp.where(valid, pltpu.roll(a, shift, 1), 1.0)
        b_sh = jnp.where(valid, pltpu.roll(b, shift, 1), 0.0)
        b = a * b_sh + b
        a = a * a_sh
        d *= 2
    a = a.reshape(tc, width)
    b = b.reshape(tc, width)
    state = carry_ref[...]
    tiles = [None] * n_tiles
    for t in (range(n_tiles - 1, -1, -1) if reverse else range(n_tiles)):
        lo = t * SUBLANES
        tiles[t] = b[lo:lo + SUBLANES, :] + a[lo:lo + SUBLANES, :] * state
        state = tiles[t][0:1, :] if reverse else tiles[t][SUBLANES - 1:SUBLANES, :]
    carry_ref[...] = state
    h = jnp.concatenate(tiles, axis=0)
    h_out_ref[...] = h
    if combine:
        g = gate_ref[...]
        gelu = 0.5 * g * (1.0 + jnp.tanh(math.sqrt(2.0 / math.pi) * (g + 0.044715 * (g * g * g))))
        y_out_ref[...] = ((hf_ref[...] + h) * gelu).astype(y_out_ref.dtype)


def _lru_call(xl, conv_w, conv_b, w_gates, b_gates, lam, h0, hf=None, gate=None, *,
              seq, reverse, tc):
    t, width = xl.shape
    batch = t // seq
    n_chunks = seq // tc
    hb = tc // CONV_HALO
    n_halo = seq // CONV_HALO
    combine = hf is not None

    def chunk_of(j):
        return (n_chunks - 1 - j) if reverse else j

    def main_idx(b, j):
        return (b * n_chunks + chunk_of(j), 0)

    def prev_idx(b, j):
        return (b * n_halo + jnp.maximum(chunk_of(j) * hb - 1, 0), 0)

    def next_idx(b, j):
        return (b * n_halo + jnp.minimum((chunk_of(j) + 1) * hb, n_halo - 1), 0)

    in_specs = [pl.BlockSpec((CONV_HALO, width), prev_idx),
                pl.BlockSpec((tc, width), main_idx),
                pl.BlockSpec((CONV_HALO, width), next_idx),
                _full(conv_w.shape), _full((1, width)), _full(w_gates.shape),
                _full((1, 2 * width)), _full((1, width)),
                pl.BlockSpec((None, 1, width), lambda b, j: (b, 0, 0))]
    args = [xl, xl, xl, conv_w, conv_b.reshape(1, width), w_gates, b_gates.reshape(1, 2 * width),
            lam.reshape(1, width), h0.reshape(batch, 1, width)]
    out_shape = [jax.ShapeDtypeStruct((t, width), F32)]
    out_specs = [pl.BlockSpec((tc, width), main_idx)]
    if combine:
        in_specs += [pl.BlockSpec((tc, width), main_idx)] * 2
        args += [hf, gate]
        out_shape.append(jax.ShapeDtypeStruct((t, width), BF16))
        out_specs.append(pl.BlockSpec((tc, width), main_idx))
    res = pl.pallas_call(
        functools.partial(_lru_kernel, reverse=reverse, combine=combine, tc=tc, n_chunks=n_chunks),
        grid=(batch, n_chunks),
        in_specs=in_specs,
        out_specs=tuple(out_specs),
        out_shape=tuple(out_shape),
        scratch_shapes=[pltpu.VMEM((1, width), F32)],
        compiler_params=_cparams(("arbitrary", "arbitrary")),
        name="lru_bwd" if reverse else "lru_fwd",
    )(*args)
    return res if combine else res[0]


def _diff_lambda(dl_ref, lam_init):
    dl = dl_ref[...]
    s01 = jnp.sum(dl[0:1, :] * dl[1:2, :], axis=-1, keepdims=True)
    s23 = jnp.sum(dl[2:3, :] * dl[3:4, :], axis=-1, keepdims=True)
    return jnp.exp(s01) - jnp.exp(s23) + lam_init


def _diff_attn_kernel(*refs, tq, tk, n_kt, lam_init):
    if n_kt:
        q_ref, kc_ref, vct_ref, k_ref, vt_ref, dl_ref, g_ref, o_ref, s_a, s_b = refs
    else:
        q_ref, kc_ref, vct_ref, dl_ref, g_ref, o_ref = refs
    q = q_ref[...]
    lane = lax.broadcasted_iota(jnp.int32, q.shape, 1)
    zero = jnp.zeros_like(q)
    qs = jnp.concatenate([jnp.where(lane < HEAD_DIM, q, zero), jnp.where(lane >= HEAD_DIM, q, zero)], axis=0)
    vw = 2 * HEAD_DIM

    def scores(kt):
        s = _dot_t(kt, qs)
        return s, jnp.max(s, axis=0, keepdims=True)

    def update(load_s, load_vt, n_keys, m_tile, m, acc):
        m_new = jnp.maximum(m, m_tile)
        alpha = jnp.exp2(m - m_new)
        kc = min(n_keys, MXU_K)
        acc = list(acc)
        for c0 in range(0, n_keys, kc):
            vt = load_vt(c0, kc)
            for t, q0 in enumerate(range(0, 2 * tq, MXU_N)):
                prev = alpha[:, q0:q0 + MXU_N] * acc[t] if c0 == 0 else acc[t]
                p = jnp.exp2(load_s(c0, kc, q0) - m_new[:, q0:q0 + MXU_N]).astype(BF16)
                acc[t] = prev + jnp.dot(vt, p, preferred_element_type=F32)
        return m_new, acc

    m = jnp.full((1, 2 * tq), -jnp.inf, F32)
    acc = [jnp.zeros((vct_ref.shape[0], MXU_N), F32) for _ in range(2 * tq // MXU_N)]
    s_c, mt_c = scores(kc_ref[...])
    n_ctx = kc_ref.shape[0]
    if n_kt:
        s_first, mt_next = scores(k_ref[0:tk, :])
        s_a[...] = s_first
    m, acc = update(lambda c0, n, q0: s_c[c0:c0 + n, q0:q0 + MXU_N], lambda c0, n: vct_ref[:, c0:c0 + n],
                    n_ctx, mt_c, m, acc)
    bufs = (s_a, s_b) if n_kt else ()
    for j in range(n_kt):
        cur = bufs[j % 2]
        mt_cur = mt_next
        if j + 1 < n_kt:
            s_next, mt_next = scores(k_ref[(j + 1) * tk:(j + 2) * tk, :])
            bufs[(j + 1) % 2][...] = s_next
        m, acc = update(lambda c0, n, q0, cur=cur: cur[c0:c0 + n, q0:q0 + MXU_N],
                        lambda c0, n, j=j: vt_ref[:, j * tk + c0:j * tk + c0 + n], tk, mt_cur, m, acc)
    acc = jnp.concatenate(acc, axis=1)
    o = acc[:vw, :] / acc[vw:vw + 1, :]
    lam = _diff_lambda(dl_ref, lam_init)
    od = o[:, :tq] - lam * o[:, tq:]
    od = od * lax.rsqrt(jnp.mean(od * od, axis=0, keepdims=True) + NORM_EPS)
    o_ref[...] = (od.T * g_ref[...] * (1.0 - lam_init)).astype(o_ref.dtype)


def _values_t(v, heads):
    batch, n, width = v.shape
    vw = width // heads
    vt = v.reshape(batch, n, heads, vw).transpose(0, 2, 3, 1)
    extra = jnp.zeros((batch, heads, VT_ROWS - vw, n), v.dtype).at[:, :, 0, :].set(1)
    return jnp.concatenate([vt, extra], axis=2)


def _diff_attn_call(q, kc, vct, k, vt, dl, g, *, seq, tq, tk, lam_init):
    t, width = q.shape
    batch = t // seq
    hw = 2 * HEAD_DIM
    heads = width // hw
    nq = seq // tq
    n_ctx = kc.shape[1]
    k_spec = lambda n: pl.BlockSpec((None, n, hw), lambda b, h, i: (b, 0, h))
    vt_spec = lambda n: pl.BlockSpec((None, None, VT_ROWS, n), lambda b, h, i: (b, h, 0, 0))
    in_specs = [pl.BlockSpec((tq, hw), lambda b, h, i: (b * nq + i, h)), k_spec(n_ctx), vt_spec(n_ctx)]
    args = [q, kc, vct]
    n_kt = 0
    if k is not None:
        n_lat = k.shape[1]
        n_kt = n_lat // tk
        in_specs += [k_spec(n_lat), vt_spec(n_lat)]
        args += [k, vt]
    in_specs += [_full(dl.shape), _full((1, hw))]
    args += [dl, g.reshape(1, hw)]
    scratch = [pltpu.VMEM((tk, 2 * tq), F32)] * 2 if n_kt else []
    return pl.pallas_call(
        functools.partial(_diff_attn_kernel, tq=tq, tk=tk, n_kt=n_kt, lam_init=lam_init),
        grid=(batch, heads, nq),
        in_specs=in_specs,
        out_specs=pl.BlockSpec((tq, hw), lambda b, h, i: (b * nq + i, h)),
        out_shape=jax.ShapeDtypeStruct((t, width), BF16),
        scratch_shapes=scratch,
        compiler_params=_cparams(("arbitrary", "arbitrary", "arbitrary")),
        name="diff_attn" if n_kt else "diff_attn_ctx",
    )(*args)


def _rpb_table_kernel(r_ref, o_ref):
    n = o_ref.shape[-1]
    j = lax.broadcasted_iota(jnp.int32, (r_ref.shape[-1], n), 0)
    pos = lax.broadcasted_iota(jnp.int32, (r_ref.shape[-1], n), 1)
    qc = pos // GRID_W
    kc = pos % GRID_W
    coff = jnp.clip(kc - qc + (NA_KW - 1), 0, 2 * NA_KW - 2)
    onehot = (coff == j).astype(F32)
    vals = jnp.dot(r_ref[...], onehot, preferred_element_type=F32, precision=lax.Precision.HIGHEST)
    pos1 = lax.broadcasted_iota(jnp.int32, vals.shape, 1)
    qc1 = pos1 // GRID_W
    kc1 = pos1 % GRID_W
    c0 = jnp.clip(qc1 - NA_KW // 2, 0, GRID_W - NA_KW)
    in_win = (kc1 >= c0) & (kc1 < c0 + NA_KW)
    o_ref[...] = jnp.where(in_win, vals * LOG2E, NEG_INF)


def _rpb_table(rpb):
    heads = rpb.shape[0]
    ncol = 2 * NA_KW
    rp = jnp.pad(rpb, ((0, 0), (0, 0), (0, ncol - rpb.shape[-1])))
    r_all = jnp.stack([rp[:, NA_KH - 1 - p:2 * NA_KH - 1 - p, :] for p in range(NA_KH)], axis=0)
    r_all = r_all.reshape(NA_KH, heads * NA_KH, ncol)
    n = GRID_W * GRID_W
    tab = pl.pallas_call(
        _rpb_table_kernel,
        grid=(NA_KH,),
        in_specs=[pl.BlockSpec((None, heads * NA_KH, ncol), lambda p: (p, 0, 0))],
        out_specs=pl.BlockSpec((None, heads * NA_KH, n), lambda p: (p, 0, 0)),
        out_shape=jax.ShapeDtypeStruct((NA_KH, heads * NA_KH, n), F32),
        compiler_params=_cparams(("arbitrary",)),
        name="rpb_table",
    )(r_all)
    tab = tab.reshape(NA_KH, heads, NA_KH, GRID_W, GRID_W).transpose(0, 1, 3, 2, 4)
    return tab.reshape(NA_KH, heads * GRID_W, NA_KH * GRID_W)


def _stack_heads(q, heads):
    head_of_lane = lax.broadcasted_iota(jnp.int32, q.shape, 1) // HEAD_DIM
    zero = jnp.zeros_like(q)
    return jnp.concatenate([jnp.where(head_of_lane == h, q, zero) for h in range(heads)], axis=0)


def _unstack_heads(res, heads):
    n = res.shape[0] // heads
    head_of_lane = lax.broadcasted_iota(jnp.int32, (n, res.shape[1]), 1) // HEAD_DIM
    out = jnp.zeros((n, res.shape[1]), res.dtype)
    for h in range(heads):
        out = jnp.where(head_of_lane == h, res[h * n:(h + 1) * n, :], out)
    return out


def _na_kernel(q_ref, k_ref, v_ref, kc_ref, vc_ref, bias_ref, o_ref, *, rows, heads, row_block):
    n_loc = NA_KH * GRID_W
    for i in range(row_block):
        r = pl.program_id(1) * row_block + i
        r0 = jnp.clip(r - NA_KH // 2, 0, rows - NA_KH)
        pat = r - r0
        start = pl.multiple_of(r0 * GRID_W, GRID_W)
        ks = k_ref[pl.ds(start, n_loc), :]
        vs = v_ref[pl.ds(start, n_loc), :]
        qs = _stack_heads(q_ref[i * GRID_W:(i + 1) * GRID_W, :], heads)
        s_loc = _dot_t(qs, ks) + bias_ref[pat]
        s_ctx = _dot_t(qs, kc_ref[...])
        m = jnp.maximum(jnp.max(s_loc, axis=-1, keepdims=True), jnp.max(s_ctx, axis=-1, keepdims=True))
        e_loc = jnp.exp2(s_loc - m)
        e_ctx = jnp.exp2(s_ctx - m)
        denom = jnp.sum(e_loc, axis=-1, keepdims=True) + jnp.sum(e_ctx, axis=-1, keepdims=True)
        res = (jnp.dot(e_loc.astype(BF16), vs, preferred_element_type=F32)
               + jnp.dot(e_ctx.astype(BF16), vc_ref[...], preferred_element_type=F32)) / denom
        o_ref[i * GRID_W:(i + 1) * GRID_W, :] = _unstack_heads(res, heads).astype(o_ref.dtype)


def _na_call(q, k, v, kc, vc, bias, *, seq, row_block):
    t, width = q.shape
    batch = t // seq
    rows = seq // GRID_W
    assert rows >= NA_KH and rows % row_block == 0, (rows, row_block)
    heads = width // HEAD_DIM
    n_ctx = kc.shape[1]
    steps = rows // row_block
    tq = row_block * GRID_W
    return pl.pallas_call(
        functools.partial(_na_kernel, rows=rows, heads=heads, row_block=row_block),
        grid=(batch, steps),
        in_specs=[pl.BlockSpec((tq, width), lambda b, r: (b * steps + r, 0)),
                  pl.BlockSpec((None, seq, width), lambda b, r: (b, 0, 0)),
                  pl.BlockSpec((None, seq, width), lambda b, r: (b, 0, 0)),
                  pl.BlockSpec((None, n_ctx, width), lambda b, r: (b, 0, 0)),
                  pl.BlockSpec((None, n_ctx, width), lambda b, r: (b, 0, 0)),
                  _full(bias.shape)],
        out_specs=pl.BlockSpec((tq, width), lambda b, r: (b * steps + r, 0)),
        out_shape=jax.ShapeDtypeStruct((t, width), BF16),
        compiler_params=_cparams(("arbitrary", "arbitrary")),
        name="na_attn",
    )(q, k, v, kc, vc, bias)


def _plain_attn_kernel(q_ref, k_ref, v_ref, o_ref, *, heads):
    qs = _stack_heads(q_ref[...], heads)
    s = _dot_t(qs, k_ref[...])
    e = jnp.exp2(s - jnp.max(s, axis=-1, keepdims=True))
    res = jnp.dot(e.astype(BF16), v_ref[...], preferred_element_type=F32) / jnp.sum(e, axis=-1, keepdims=True)
    o_ref[...] = _unstack_heads(res, heads).astype(o_ref.dtype)


def _plain_attn_call(q, k, v, *, seq):
    t, width = q.shape
    spec = pl.BlockSpec((seq, width), lambda b: (b, 0))
    return pl.pallas_call(
        functools.partial(_plain_attn_kernel, heads=width // HEAD_DIM),
        grid=(t // seq,),
        in_specs=[spec, spec, spec],
        out_specs=spec,
        out_shape=jax.ShapeDtypeStruct((t, width), BF16),
        compiler_params=_cparams(("arbitrary",)),
        name="ctx_plain_attn",
    )(q, k, v)


def _out_ffn_kernel(*refs, widths, hidden, h_chunk, final):
    (x_ref, lru_ref, diff_ref, na_ref, wo_ref, g1_ref, n2_ref, sh_ref, sc_ref, g2_ref,
     wgu_ref, wd_ref) = refs[:12]
    if final:
        fg_ref, o_ref = refs[12:]
    else:
        o_ref = refs[12]
    off = 0
    mix = None
    for part_ref, wd_ in zip((lru_ref, diff_ref, na_ref), widths):
        term = jnp.dot(part_ref[...], wo_ref[off:off + wd_, :], preferred_element_type=F32)
        mix = term if mix is None else mix + term
        off += wd_
    x1 = x_ref[...] + g1_ref[...] * mix
    hb = (_rms(x1, n2_ref[...]) * (1.0 + sc_ref[...]) + sh_ref[...]).astype(BF16)
    acc = None
    for c0 in range(0, hidden, h_chunk):
        g = jnp.dot(hb, wgu_ref[:, c0:c0 + h_chunk], preferred_element_type=F32)
        u = jnp.dot(hb, wgu_ref[:, hidden + c0:hidden + c0 + h_chunk], preferred_element_type=F32)
        a = (g * _sigmoid(g) * u).astype(BF16)
        term = jnp.dot(a, wd_ref[c0:c0 + h_chunk, :], preferred_element_type=F32)
        acc = term if acc is None else acc + term
    x2 = x1 + g2_ref[...] * acc
    if final:
        x2 = _rms(x2, fg_ref[...])
    o_ref[...] = x2


def _out_ffn_call(xs, lru, diff, na, w_out, n2g, mods, w_gu, w_down, final_g,
                  *, layer, shared_row, rows_per_batch, tm, h_chunk):
    t, d = xs.shape
    hidden = w_down.shape[1]
    per_batch = rows_per_batch // tm
    mod = lambda k: _mod_spec(mods, layer, k, per_batch, shared_row)
    row = lambda wd_: pl.BlockSpec((tm, wd_), lambda i: (i, 0))
    widths = (lru.shape[1], diff.shape[1], na.shape[1])
    in_specs = [row(d), row(widths[0]), row(widths[1]), row(widths[2]), _layer(w_out, layer), mod(2),
                _layer(n2g, layer), mod(3), mod(4), mod(5), _layer(w_gu, layer), _layer(w_down, layer)]
    args = [xs, lru, diff, na, w_out, mods, n2g, mods, mods, mods, w_gu, w_down]
    final = final_g is not None
    if final:
        in_specs.append(_full((1, d)))
        args.append(final_g)
    return pl.pallas_call(
        functools.partial(_out_ffn_kernel, widths=widths, hidden=hidden, h_chunk=h_chunk, final=final),
        grid=(t // tm,),
        in_specs=in_specs,
        out_specs=row(d),
        out_shape=jax.ShapeDtypeStruct((t, d), F32),
        compiler_params=_cparams(("arbitrary",)),
        name="out_ffn_final" if final else "out_ffn",
    )(*args)


def _rope_tables(seq):
    tpos = jnp.arange(seq, dtype=jnp.int32)
    row = (tpos // GRID_W).astype(F32)
    col = (tpos % GRID_W).astype(F32)
    axis_dim = HEAD_DIM // 2
    inv_freq = 1.0 / (ROPE_BASE ** (jnp.arange(0, axis_dim, 2, dtype=F32) / axis_dim))
    ang_r = row[:, None] * inv_freq[None]
    ang_c = col[:, None] * inv_freq[None]
    cos = jnp.concatenate([jnp.cos(ang_r)] * 2 + [jnp.cos(ang_c)] * 2, axis=-1)
    sin = jnp.concatenate([-jnp.sin(ang_r), jnp.sin(ang_r), -jnp.sin(ang_c), jnp.sin(ang_c)], axis=-1)
    return jnp.concatenate([cos, cos], axis=-1), jnp.concatenate([sin, sin], axis=-1)


def _pick_tile(n, target):
    t = min(n, target)
    while n % t:
        t //= 2
    return t


def kernel(x, c, ctx, c_ctx, w_mod, b_mod, norm1_g, norm2_g, w_in, lru_conv_w, lru_conv_b, lru_wa, lru_ba,
           lru_wx, lru_bx, lru_lambda, diff_lambda, diff_subln_g, na_rpb, w_out, ffn_w_gu, ffn_w_down, final_g):
    batch, seq, d = x.shape
    n_ctx = ctx.shape[1]
    depth = w_mod.shape[0]
    lru_w = lru_conv_w.shape[-1]
    diff_w = d // 2
    na_w = d - lru_w - diff_w
    sections = (lru_w, lru_w, diff_w, diff_w, diff_w, na_w, na_w, na_w)
    assert sum(sections) == w_in.shape[-1], (sections, w_in.shape)
    hidden = ffn_w_down.shape[1]

    cvec = jnp.concatenate([c, c_ctx[None], jnp.zeros((MOD_ROWS - batch - 1, d), F32)], axis=0)
    mods = _mod_call(cvec, w_mod, b_mod).reshape(depth * MOD_ROWS * 6, 1, d)
    rope_tabs = _rope_tables(seq)

    w_in_b = w_in.astype(BF16)
    w_out_b = w_out.astype(BF16)
    w_gu_b = ffn_w_gu.astype(BF16)
    w_down_b = ffn_w_down.astype(BF16)
    n1g = norm1_g.reshape(depth, 1, d)
    n2g = norm2_g.reshape(depth, 1, d)
    fg = final_g.reshape(1, d)
    eye = jnp.eye(LRU_BLOCKS, dtype=F32)[None, None, :, None, :, None]
    dense = lambda w: (eye * w[:, :, :, :, None, :]).reshape(depth, 2, lru_w, lru_w)
    w_gates = jnp.concatenate([dense(lru_wa), dense(lru_wx)], axis=-1).astype(BF16)
    b_gates = jnp.concatenate([lru_ba, lru_bx], axis=-1)

    xs = x.reshape(batch * seq, d)
    hs = ctx.reshape(batch * n_ctx, d)
    tm = _pick_tile(seq, 512)
    tm_c = _pick_tile(n_ctx, 256)
    tc = _pick_tile(seq, 512)
    tc_c = _pick_tile(n_ctx, 256)
    zeros_h = jnp.zeros((batch, lru_w), F32)
    lat = dict(shared_row=None, rows_per_batch=seq, tm=tm)
    con = dict(shared_row=batch, rows_per_batch=n_ctx, tm=tm_c)

    for l in range(depth):
        last = l == depth - 1
        lam_init = 0.8 - 0.6 * math.exp(-0.3 * l)
        px = _in_proj_call(xs, n1g, mods, w_in_b, rope_tabs, layer=l, sections=sections, **lat)
        pc = _in_proj_call(hs, n1g, mods, w_in_b, None, layer=l, sections=sections, **con)
        xl, gl, qd, kd, vd, qn, kn, vn = px
        cxl, cgl, cqd, ckd, cvd, cqn, ckn, cvn = pc

        lru_f = (lru_conv_w[l], lru_conv_b[l], w_gates[l, 0], b_gates[l, 0], lru_lambda[l, 0])
        lru_b = (lru_conv_w[l], lru_conv_b[l], w_gates[l, 1], b_gates[l, 1], lru_lambda[l, 1])
        hc_f = _lru_call(cxl, *lru_f, zeros_h, seq=n_ctx, reverse=False, tc=tc_c)
        hc_b, lru_c = _lru_call(cxl, *lru_b, zeros_h, hc_f, cgl, seq=n_ctx, reverse=True, tc=tc_c)
        hcf_last = hc_f.reshape(batch, n_ctx, lru_w)[:, -1]
        hcb_first = hc_b.reshape(batch, n_ctx, lru_w)[:, 0]
        hl_f = _lru_call(xl, *lru_f, hcf_last, seq=seq, reverse=False, tc=tc)
        _, lru_x = _lru_call(xl, *lru_b, hcb_first, hl_f, gl, seq=seq, reverse=True, tc=tc)

        ckd3 = ckd.reshape(batch, n_ctx, diff_w)
        diff_heads = diff_w // (2 * HEAD_DIM)
        cvdt = _values_t(cvd.reshape(batch, n_ctx, diff_w), diff_heads)
        vdt = _values_t(vd.reshape(batch, seq, diff_w), diff_heads)
        diff_x = _diff_attn_call(qd, ckd3, cvdt, kd.reshape(batch, seq, diff_w), vdt,
                                 diff_lambda[l], diff_subln_g[l], seq=seq, tq=_pick_tile(seq, 512),
                                 tk=_pick_tile(seq // 4, 2048), lam_init=lam_init)

        bias = _rpb_table(na_rpb[l])
        ckn3 = ckn.reshape(batch, n_ctx, na_w)
        cvn3 = cvn.reshape(batch, n_ctx, na_w)
        na_x = _na_call(qn, kn.reshape(batch, seq, na_w), vn.reshape(batch, seq, na_w), ckn3, cvn3, bias, seq=seq,
                        row_block=8)

        xs = _out_ffn_call(xs, lru_x, diff_x, na_x, w_out_b, n2g, mods, w_gu_b, w_down_b, fg if last else None,
                           layer=l, h_chunk=hidden // 2, **lat)
        if not last:
            diff_c = _diff_attn_call(cqd, ckd3, cvdt, None, None, diff_lambda[l], diff_subln_g[l],
                                     seq=n_ctx, tq=n_ctx, tk=n_ctx, lam_init=lam_init)
            na_c = _plain_attn_call(cqn, ckn, cvn, seq=n_ctx)
            hs = _out_ffn_call(hs, lru_c, diff_c, na_c, w_out_b, n2g, mods, w_gu_b, w_down_b, None,
                               layer=l, h_chunk=hidden // 2, **con)
    return xs.reshape(batch, seq, d)
```

```python
import functools
import math

import jax
import jax.numpy as jnp
from jax import lax
from jax.experimental import pallas as pl
from jax.experimental.pallas import tpu as pltpu

GRID_W = 64
HEAD_DIM = 64
LRU_BLOCKS = 4
LRU_CONV = 4
LRU_C = 8.0
NA_KH = 8
NA_KW = 16
ROPE_BASE = 10000.0
NORM_EPS = 1e-6
NEG_INF = -1e30
LOG2E = math.log2(math.e)
MXU_K = 256
SUBLANES = 8
MOD_ROWS = 8
CONV_HALO = 8
VMEM_LIMIT = 56 * 1024 * 1024

F32 = jnp.float32
BF16 = jnp.bfloat16


def _cparams(sem):
    return pltpu.CompilerParams(dimension_semantics=sem, vmem_limit_bytes=VMEM_LIMIT)


def _full(shape):
    nd = len(shape)
    return pl.BlockSpec(shape, lambda *_: (0,) * nd, pipeline_mode=pl.Buffered(1))


def _layer(arr, l):
    tail = arr.shape[1:]
    return pl.BlockSpec((None,) + tail, lambda *_: (l,) + (0,) * len(tail), pipeline_mode=pl.Buffered(1))


def _mod_spec(mods, l, k, per_batch, shared_row):
    d = mods.shape[-1]
    base = (l * MOD_ROWS) * 6 + k
    if shared_row is not None:
        return pl.BlockSpec((None, 1, d), lambda i: (base + shared_row * 6, 0, 0))
    return pl.BlockSpec((None, 1, d), lambda i: (base + (i // per_batch) * 6, 0, 0))


def _rms(x, g):
    return x * lax.rsqrt(jnp.mean(x * x, axis=-1, keepdims=True) + NORM_EPS) * g


def _sigmoid(x):
    return 1.0 / (1.0 + jnp.exp(-x))


def _dot_t(a, b):
    return lax.dot_general(a, b, (((1,), (1,)), ((), ())), preferred_element_type=F32)


def _mod_kernel(c_ref, w_ref, b_ref, o_ref):
    cv = c_ref[...]
    cond = cv * _sigmoid(cv)
    o_ref[...] = jnp.dot(cond, w_ref[...], preferred_element_type=F32,
                         precision=lax.Precision.HIGHEST) + b_ref[...]


def _mod_call(cvec, w_mod, b_mod):
    depth, d, n = w_mod.shape
    rows = cvec.shape[0]
    tn = d
    return pl.pallas_call(
        _mod_kernel,
        grid=(depth, n // tn),
        in_specs=[pl.BlockSpec((rows, d), lambda l, j: (0, 0)),
                  pl.BlockSpec((None, d, tn), lambda l, j: (l, 0, j)),
                  pl.BlockSpec((None, 1, tn), lambda l, j: (l, 0, j))],
        out_specs=pl.BlockSpec((None, rows, tn), lambda l, j: (l, 0, j)),
        out_shape=jax.ShapeDtypeStruct((depth, rows, n), F32),
        compiler_params=_cparams(("arbitrary", "arbitrary")),
        name="mod",
    )(cvec, w_mod, b_mod.reshape(depth, 1, n))


def _rope(p, cos, sin_signed):
    n = p.shape[-1]
    lane = lax.broadcasted_iota(jnp.int32, p.shape, 1)
    low = (lane % 32) < 16
    partner = jnp.where(low, pltpu.roll(p, n - 16, 1), pltpu.roll(p, 16, 1))
    return p * cos + partner * sin_signed


def _in_proj_kernel(*refs, sections, rope, q_scale):
    if rope:
        x_ref, g_ref, sh_ref, sc_ref, w_ref, cos_ref, sin_ref = refs[:7]
        outs = refs[7:]
    else:
        x_ref, g_ref, sh_ref, sc_ref, w_ref = refs[:5]
        outs = refs[5:]
    h = _rms(x_ref[...], g_ref[...]) * (1.0 + sc_ref[...]) + sh_ref[...]
    hb = h.astype(BF16)
    off = 0
    for idx, (width, o_ref) in enumerate(zip(sections, outs)):
        p = jnp.dot(hb, w_ref[:, off:off + width], preferred_element_type=F32)
        if rope and idx in (2, 3):
            reps = width // cos_ref.shape[-1]
            cos = jnp.concatenate([cos_ref[...]] * reps, axis=-1)
            sin = jnp.concatenate([sin_ref[...]] * reps, axis=-1)
            p = _rope(p, cos, sin)
        if idx in (2, 5):
            p = p * q_scale
        o_ref[...] = p.astype(o_ref.dtype)
        off += width


def _in_proj_call(xs, g, mods, w, rope_tabs, *, layer, shared_row, sections, rows_per_batch, tm):
    t, d = xs.shape
    per_batch = rows_per_batch // tm
    in_specs = [pl.BlockSpec((tm, d), lambda i: (i, 0)),
                _layer(g, layer),
                _mod_spec(mods, layer, 0, per_batch, shared_row),
                _mod_spec(mods, layer, 1, per_batch, shared_row),
                _layer(w, layer)]
    args = [xs, g, mods, mods, w]
    rope = rope_tabs is not None
    if rope:
        cos, sin = rope_tabs
        tw = cos.shape[-1]
        in_specs += [pl.BlockSpec((tm, tw), lambda i: (i % per_batch, 0))] * 2
        args += [cos, sin]
    dtypes = (F32, F32, BF16, BF16, BF16, BF16, BF16, BF16)
    out_shape = tuple(jax.ShapeDtypeStruct((t, wd), dt) for wd, dt in zip(sections, dtypes))
    out_specs = tuple(pl.BlockSpec((tm, wd), lambda i: (i, 0)) for wd in sections)
    return pl.pallas_call(
        functools.partial(_in_proj_kernel, sections=sections, rope=rope, q_scale=HEAD_DIM ** -0.5 * LOG2E),
        grid=(t // tm,),
        in_specs=in_specs,
        out_specs=out_specs,
        out_shape=out_shape,
        compiler_params=_cparams(("arbitrary",)),
        name="in_proj_rope" if rope else "in_proj",
    )(*args)


def _lru_kernel(*refs, reverse, combine, tc, n_chunks):
    if combine:
        (prev_ref, x_ref, next_ref, cw_ref, cb_ref, w_ref, bias_ref, lam_ref, h0_ref,
         hf_ref, gate_ref, h_out_ref, y_out_ref, carry_ref) = refs
    else:
        (prev_ref, x_ref, next_ref, cw_ref, cb_ref, w_ref, bias_ref, lam_ref, h0_ref,
         h_out_ref, carry_ref) = refs
    j = pl.program_id(1)
    chunk = (n_chunks - 1 - j) if reverse else j

    @pl.when(j == 0)
    def _():
        carry_ref[...] = h0_ref[...]

    width = x_ref.shape[-1]
    has_prev = (chunk > 0).astype(F32)
    has_next = (chunk < n_chunks - 1).astype(F32)
    xcat = jnp.concatenate([prev_ref[...] * has_prev, x_ref[...], next_ref[...] * has_next], axis=0)
    left = LRU_CONV // 2
    u = cb_ref[...]
    for tap in range(LRU_CONV):
        start = CONV_HALO - left + tap
        u = u + cw_ref[tap:tap + 1, :] * xcat[start:start + tc, :]

    gates = jnp.dot(u.astype(BF16), w_ref[...], preferred_element_type=F32) + bias_ref[...]
    r = _sigmoid(gates[:, :width])
    i = _sigmoid(gates[:, width:])
    lam = lam_ref[...]
    log_sig = jnp.minimum(lam, 0.0) - jnp.log(1.0 + jnp.exp(-jnp.abs(lam)))
    log_a = LRU_C * r * log_sig
    a = jnp.exp(log_a)
    b = jnp.sqrt(1.0 - jnp.exp(2.0 * log_a)) * (i * u)

    n_tiles = tc // SUBLANES
    a = a.reshape(n_tiles, SUBLANES, width)
    b = b.reshape(n_tiles, SUBLANES, width)
    row = lax.broadcasted_iota(jnp.int32, a.shape, 1)
    d = 1
    while d < SUBLANES:
        if reverse:
            valid = row < SUBLANES - d
            shift = SUBLANES - d
        else:
            valid = row >= d
            shift = d
        a_sh = jnp.where(valid, pltpu.roll(a, shift, 1), 1.0)
        b_sh = jnp.where(valid, pltpu.roll(b, shift, 1), 0.0)
        b = a * b_sh + b
        a = a * a_sh
        d *= 2
    a = a.reshape(tc, width)
    b = b.reshape(tc, width)
    state = carry_ref[...]
    tiles = [None] * n_tiles
    for t in (range(n_tiles - 1, -1, -1) if reverse else range(n_tiles)):
        lo = t * SUBLANES
        tiles[t] = b[lo:lo + SUBLANES, :] + a[lo:lo + SUBLANES, :] * state
        state = tiles[t][0:1, :] if reverse else tiles[t][SUBLANES - 1:SUBLANES, :]
    carry_ref[...] = state
    h = jnp.concatenate(tiles, axis=0)
    h_out_ref[...] = h
    if combine:
        g = gate_ref[...]
        gelu = 0.5 * g * (1.0 + jnp.tanh(math.sqrt(2.0 / math.pi) * (g + 0.044715 * (g * g * g))))
        y_out_ref[...] = ((hf_ref[...] + h) * gelu).astype(y_out_ref.dtype)


def _lru_call(xl, conv_w, conv_b, w_gates, b_gates, lam, h0, hf=None, gate=None, *,
              seq, reverse, tc):
    t, width = xl.shape
    batch = t // seq
    n_chunks = seq // tc
    hb = tc // CONV_HALO
    n_halo = seq // CONV_HALO
    combine = hf is not None

    def chunk_of(j):
        return (n_chunks - 1 - j) if reverse else j

    def main_idx(b, j):
        return (b * n_chunks + chunk_of(j), 0)

    def prev_idx(b, j):
        return (b * n_halo + jnp.maximum(chunk_of(j) * hb - 1, 0), 0)

    def next_idx(b, j):
        return (b * n_halo + jnp.minimum((chunk_of(j) + 1) * hb, n_halo - 1), 0)

    in_specs = [pl.BlockSpec((CONV_HALO, width), prev_idx),
                pl.BlockSpec((tc, width), main_idx),
                pl.BlockSpec((CONV_HALO, width), next_idx),
                _full(conv_w.shape), _full((1, width)), _full(w_gates.shape),
                _full((1, 2 * width)), _full((1, width)),
                pl.BlockSpec((None, 1, width), lambda b, j: (b, 0, 0))]
    args = [xl, xl, xl, conv_w, conv_b.reshape(1, width), w_gates, b_gates.reshape(1, 2 * width),
            lam.reshape(1, width), h0.reshape(batch, 1, width)]
    out_shape = [jax.ShapeDtypeStruct((t, width), F32)]
    out_specs = [pl.BlockSpec((tc, width), main_idx)]
    if combine:
        in_specs += [pl.BlockSpec((tc, width), main_idx)] * 2
        args += [hf, gate]
        out_shape.append(jax.ShapeDtypeStruct((t, width), BF16))
        out_specs.append(pl.BlockSpec((tc, width), main_idx))
    res = pl.pallas_call(
        functools.partial(_lru_kernel, reverse=reverse, combine=combine, tc=tc, n_chunks=n_chunks),
        grid=(batch, n_chunks),
        in_specs=in_specs,
        out_specs=tuple(out_specs),
        out_shape=tuple(out_shape),
        scratch_shapes=[pltpu.VMEM((1, width), F32)],
        compiler_params=_cparams(("arbitrary", "arbitrary")),
        name="lru_bwd" if reverse else "lru_fwd",
    )(*args)
    return res if combine else res[0]


def _diff_lambda(dl_ref, lam_init):
    dl = dl_ref[...]
    s01 = jnp.sum(dl[0:1, :] * dl[1:2, :], axis=-1, keepdims=True)
    s23 = jnp.sum(dl[2:3, :] * dl[3:4, :], axis=-1, keepdims=True)
    return jnp.exp(s01) - jnp.exp(s23) + lam_init


def _diff_attn_kernel(*refs, tq, tk, n_kt, lam_init):
    if n_kt:
        q_ref, kc_ref, vc_ref, k_ref, v_ref, dl_ref, g_ref, o_ref, s_a, s_b = refs
    else:
        q_ref, kc_ref, vc_ref, dl_ref, g_ref, o_ref = refs
    q = q_ref[...]
    lane = lax.broadcasted_iota(jnp.int32, q.shape, 1)
    zero = jnp.zeros_like(q)
    qs = jnp.concatenate([jnp.where(lane < HEAD_DIM, q, zero), jnp.where(lane >= HEAD_DIM, q, zero)], axis=0)
    vw = vc_ref.shape[-1]

    def scores(kt):
        s = _dot_t(qs, kt)
        return s, jnp.max(s, axis=-1, keepdims=True)

    def update(load_s, vt, m_tile, m, acc):
        m_new = jnp.maximum(m, m_tile)
        acc = jnp.exp2(m - m_new) * acc
        v_ext = jnp.concatenate([vt, jnp.ones_like(vt)], axis=1)
        n_keys = vt.shape[0]
        kc = min(n_keys, MXU_K)
        for c0 in range(0, n_keys, kc):
            p = jnp.exp2(load_s(c0, kc) - m_new).astype(BF16)
            acc = acc + jnp.dot(p, v_ext[c0:c0 + kc, :], preferred_element_type=F32)
        return m_new, acc

    def k_tile(j):
        return k_ref[pl.ds(pl.multiple_of(j * tk, tk), tk), :]

    def v_tile(j):
        return v_ref[pl.ds(pl.multiple_of(j * tk, tk), tk), :]

    m = jnp.full((2 * tq, 1), -jnp.inf, F32)
    acc = jnp.zeros((2 * tq, 2 * vw), F32)
    if n_kt:
        s_first, mt_a = scores(k_tile(0))
        s_a[...] = s_first

        def pair(j, carry, last):
            m, acc, mt_a = carry
            s_next, mt_b = scores(k_tile(j + 1))
            s_b[...] = s_next
            m, acc = update(lambda c0, n: s_a[:, c0:c0 + n], v_tile(j), mt_a, m, acc)
            if last:
                s_ctx, mt_a = scores(kc_ref[...])
            else:
                s_ctx = None
                s_next, mt_a = scores(k_tile(j + 2))
                s_a[...] = s_next
            m, acc = update(lambda c0, n: s_b[:, c0:c0 + n], v_tile(j + 1), mt_b, m, acc)
            return (m, acc, mt_a), s_ctx

        carry = lax.fori_loop(0, n_kt // 2 - 1, lambda t, c: pair(2 * t, c, False)[0], (m, acc, mt_a))
        (m, acc, mt_c), s_c = pair(n_kt - 2, carry, True)
    else:
        s_c, mt_c = scores(kc_ref[...])
    m, acc = update(lambda c0, n: s_c[:, c0:c0 + n], vc_ref[...], mt_c, m, acc)
    o = acc[:, :vw] / acc[:, vw:vw + 1]
    lam = _diff_lambda(dl_ref, lam_init)
    od = o[:tq, :] - lam * o[tq:, :]
    o_ref[...] = (_rms(od, g_ref[...]) * (1.0 - lam_init)).astype(o_ref.dtype)


def _diff_attn_call(q, kc, vc, k, v, dl, g, *, seq, tq, tk, lam_init):
    t, width = q.shape
    batch = t // seq
    hw = 2 * HEAD_DIM
    heads = width // hw
    nq = seq // tq
    n_ctx = kc.shape[1]
    kv_spec = lambda n: pl.BlockSpec((None, n, hw), lambda b, h, i: (b, 0, h))
    in_specs = [pl.BlockSpec((tq, hw), lambda b, h, i: (b * nq + i, h)), kv_spec(n_ctx), kv_spec(n_ctx)]
    args = [q, kc, vc]
    n_kt = 0
    if k is not None:
        n_lat = k.shape[1]
        n_kt = n_lat // tk
        in_specs += [kv_spec(n_lat), kv_spec(n_lat)]
        args += [k, v]
    in_specs += [_full(dl.shape), _full((1, hw))]
    args += [dl, g.reshape(1, hw)]
    assert n_kt % 2 == 0, n_kt
    scratch = [pltpu.VMEM((2 * tq, tk), F32)] * 2 if n_kt else []
    return pl.pallas_call(
        functools.partial(_diff_attn_kernel, tq=tq, tk=tk, n_kt=n_kt, lam_init=lam_init),
        grid=(batch, heads, nq),
        in_specs=in_specs,
        out_specs=pl.BlockSpec((tq, hw), lambda b, h, i: (b * nq + i, h)),
        out_shape=jax.ShapeDtypeStruct((t, width), BF16),
        scratch_shapes=scratch,
        compiler_params=_cparams(("arbitrary", "arbitrary", "arbitrary")),
        name="diff_attn" if n_kt else "diff_attn_ctx",
    )(*args)


def _rpb_table_kernel(r_ref, o_ref):
    n = o_ref.shape[-1]
    j = lax.broadcasted_iota(jnp.int32, (r_ref.shape[-1], n), 0)
    pos = lax.broadcasted_iota(jnp.int32, (r_ref.shape[-1], n), 1)
    qc = pos // GRID_W
    kc = pos % GRID_W
    coff = jnp.clip(kc - qc + (NA_KW - 1), 0, 2 * NA_KW - 2)
    onehot = (coff == j).astype(F32)
    vals = jnp.dot(r_ref[...], onehot, preferred_element_type=F32, precision=lax.Precision.HIGHEST)
    pos1 = lax.broadcasted_iota(jnp.int32, vals.shape, 1)
    qc1 = pos1 // GRID_W
    kc1 = pos1 % GRID_W
    c0 = jnp.clip(qc1 - NA_KW // 2, 0, GRID_W - NA_KW)
    in_win = (kc1 >= c0) & (kc1 < c0 + NA_KW)
    o_ref[...] = jnp.where(in_win, vals * LOG2E, NEG_INF)


def _rpb_table(rpb):
    heads = rpb.shape[0]
    ncol = 2 * NA_KW
    rp = jnp.pad(rpb, ((0, 0), (0, 0), (0, ncol - rpb.shape[-1])))
    r_all = jnp.stack([rp[:, NA_KH - 1 - p:2 * NA_KH - 1 - p, :] for p in range(NA_KH)], axis=0)
    r_all = r_all.reshape(NA_KH, heads * NA_KH, ncol)
    n = GRID_W * GRID_W
    tab = pl.pallas_call(
        _rpb_table_kernel,
        grid=(NA_KH,),
        in_specs=[pl.BlockSpec((None, heads * NA_KH, ncol), lambda p: (p, 0, 0))],
        out_specs=pl.BlockSpec((None, heads * NA_KH, n), lambda p: (p, 0, 0)),
        out_shape=jax.ShapeDtypeStruct((NA_KH, heads * NA_KH, n), F32),
        compiler_params=_cparams(("arbitrary",)),
        name="rpb_table",
    )(r_all)
    tab = tab.reshape(NA_KH, heads, NA_KH, GRID_W, GRID_W).transpose(0, 1, 3, 2, 4)
    return tab.reshape(NA_KH, heads * GRID_W, NA_KH * GRID_W)


def _stack_heads(q, heads):
    head_of_lane = lax.broadcasted_iota(jnp.int32, q.shape, 1) // HEAD_DIM
    zero = jnp.zeros_like(q)
    return jnp.concatenate([jnp.where(head_of_lane == h, q, zero) for h in range(heads)], axis=0)


def _unstack_heads(res, heads):
    n = res.shape[0] // heads
    head_of_lane = lax.broadcasted_iota(jnp.int32, (n, res.shape[1]), 1) // HEAD_DIM
    out = jnp.zeros((n, res.shape[1]), res.dtype)
    for h in range(heads):
        out = jnp.where(head_of_lane == h, res[h * n:(h + 1) * n, :], out)
    return out


def _na_kernel(q_ref, k_ref, v_ref, kc_ref, vc_ref, bias_ref, o_ref, *, rows, heads, row_block):
    n_loc = NA_KH * GRID_W
    for i in range(row_block):
        r = pl.program_id(1) * row_block + i
        r0 = jnp.clip(r - NA_KH // 2, 0, rows - NA_KH)
        pat = r - r0
        start = pl.multiple_of(r0 * GRID_W, GRID_W)
        ks = k_ref[pl.ds(start, n_loc), :]
        vs = v_ref[pl.ds(start, n_loc), :]
        qs = _stack_heads(q_ref[i * GRID_W:(i + 1) * GRID_W, :], heads)
        s_loc = _dot_t(qs, ks) + bias_ref[pat]
        s_ctx = _dot_t(qs, kc_ref[...])
        m = jnp.maximum(jnp.max(s_loc, axis=-1, keepdims=True), jnp.max(s_ctx, axis=-1, keepdims=True))
        e_loc = jnp.exp2(s_loc - m)
        e_ctx = jnp.exp2(s_ctx - m)
        denom = jnp.sum(e_loc, axis=-1, keepdims=True) + jnp.sum(e_ctx, axis=-1, keepdims=True)
        res = (jnp.dot(e_loc.astype(BF16), vs, preferred_element_type=F32)
               + jnp.dot(e_ctx.astype(BF16), vc_ref[...], preferred_element_type=F32)) / denom
        o_ref[i * GRID_W:(i + 1) * GRID_W, :] = _unstack_heads(res, heads).astype(o_ref.dtype)


def _na_call(q, k, v, kc, vc, bias, *, seq, row_block):
    t, width = q.shape
    batch = t // seq
    rows = seq // GRID_W
    assert rows >= NA_KH and rows % row_block == 0, (rows, row_block)
    heads = width // HEAD_DIM
    n_ctx = kc.shape[1]
    steps = rows // row_block
    tq = row_block * GRID_W
    return pl.pallas_call(
        functools.partial(_na_kernel, rows=rows, heads=heads, row_block=row_block),
        grid=(batch, steps),
        in_specs=[pl.BlockSpec((tq, width), lambda b, r: (b * steps + r, 0)),
                  pl.BlockSpec((None, seq, width), lambda b, r: (b, 0, 0)),
                  pl.BlockSpec((None, seq, width), lambda b, r: (b, 0, 0)),
                  pl.BlockSpec((None, n_ctx, width), lambda b, r: (b, 0, 0)),
                  pl.BlockSpec((None, n_ctx, width), lambda b, r: (b, 0, 0)),
                  _full(bias.shape)],
        out_specs=pl.BlockSpec((tq, width), lambda b, r: (b * steps + r, 0)),
        out_shape=jax.ShapeDtypeStruct((t, width), BF16),
        compiler_params=_cparams(("arbitrary", "arbitrary")),
        name="na_attn",
    )(q, k, v, kc, vc, bias)


def _plain_attn_kernel(q_ref, k_ref, v_ref, o_ref, *, heads):
    qs = _stack_heads(q_ref[...], heads)
    s = _dot_t(qs, k_ref[...])
    e = jnp.exp2(s - jnp.max(s, axis=-1, keepdims=True))
    res = jnp.dot(e.astype(BF16), v_ref[...], preferred_element_type=F32) / jnp.sum(e, axis=-1, keepdims=True)
    o_ref[...] = _unstack_heads(res, heads).astype(o_ref.dtype)


def _plain_attn_call(q, k, v, *, seq):
    t, width = q.shape
    spec = pl.BlockSpec((seq, width), lambda b: (b, 0))
    return pl.pallas_call(
        functools.partial(_plain_attn_kernel, heads=width // HEAD_DIM),
        grid=(t // seq,),
        in_specs=[spec, spec, spec],
        out_specs=spec,
        out_shape=jax.ShapeDtypeStruct((t, width), BF16),
        compiler_params=_cparams(("arbitrary",)),
        name="ctx_plain_attn",
    )(q, k, v)


def _out_ffn_kernel(*refs, widths, hidden, h_chunk, final):
    (x_ref, lru_ref, diff_ref, na_ref, wo_ref, g1_ref, n2_ref, sh_ref, sc_ref, g2_ref,
     wgu_ref, wd_ref) = refs[:12]
    if final:
        fg_ref, o_ref = refs[12:]
    else:
        o_ref = refs[12]
    off = 0
    mix = None
    for part_ref, wd_ in zip((lru_ref, diff_ref, na_ref), widths):
        term = jnp.dot(part_ref[...], wo_ref[off:off + wd_, :], preferred_element_type=F32)
        mix = term if mix is None else mix + term
        off += wd_
    x1 = x_ref[...] + g1_ref[...] * mix
    hb = (_rms(x1, n2_ref[...]) * (1.0 + sc_ref[...]) + sh_ref[...]).astype(BF16)
    acc = None
    for c0 in range(0, hidden, h_chunk):
        c1 = min(c0 + h_chunk, hidden)
        g = jnp.dot(hb, wgu_ref[:, c0:c1], preferred_element_type=F32)
        u = jnp.dot(hb, wgu_ref[:, hidden + c0:hidden + c1], preferred_element_type=F32)
        a = (g * _sigmoid(g) * u).astype(BF16)
        term = jnp.dot(a, wd_ref[c0:c1, :], preferred_element_type=F32)
        acc = term if acc is None else acc + term
    x2 = x1 + g2_ref[...] * acc
    if final:
        x2 = _rms(x2, fg_ref[...])
    o_ref[...] = x2


def _out_ffn_call(xs, lru, diff, na, w_out, n2g, mods, w_gu, w_down, final_g,
                  *, layer, shared_row, rows_per_batch, tm, h_chunk):
    t, d = xs.shape
    hidden = w_down.shape[1]
    per_batch = rows_per_batch // tm
    mod = lambda k: _mod_spec(mods, layer, k, per_batch, shared_row)
    row = lambda wd_: pl.BlockSpec((tm, wd_), lambda i: (i, 0))
    widths = (lru.shape[1], diff.shape[1], na.shape[1])
    in_specs = [row(d), row(widths[0]), row(widths[1]), row(widths[2]), _layer(w_out, layer), mod(2),
                _layer(n2g, layer), mod(3), mod(4), mod(5), _layer(w_gu, layer), _layer(w_down, layer)]
    args = [xs, lru, diff, na, w_out, mods, n2g, mods, mods, mods, w_gu, w_down]
    final = final_g is not None
    if final:
        in_specs.append(_full((1, d)))
        args.append(final_g)
    return pl.pallas_call(
        functools.partial(_out_ffn_kernel, widths=widths, hidden=hidden, h_chunk=h_chunk, final=final),
        grid=(t // tm,),
        in_specs=in_specs,
        out_specs=row(d),
        out_shape=jax.ShapeDtypeStruct((t, d), F32),
        compiler_params=_cparams(("arbitrary",)),
        name="out_ffn_final" if final else "out_ffn",
    )(*args)


def _rope_tables(seq):
    rows = seq // GRID_W
    axis_dim = HEAD_DIM // 2
    inv_freq = 1.0 / (ROPE_BASE ** (jnp.arange(0, axis_dim, 2, dtype=F32) / axis_dim))
    ang_r = jnp.arange(rows, dtype=F32)[:, None] * inv_freq[None]
    ang_c = jnp.arange(GRID_W, dtype=F32)[:, None] * inv_freq[None]
    per_row = lambda a: jnp.broadcast_to(a[:, None, :], (rows, GRID_W, a.shape[-1])).reshape(seq, -1)
    per_col = lambda a: jnp.broadcast_to(a[None, :, :], (rows, GRID_W, a.shape[-1])).reshape(seq, -1)
    cos_r, sin_r = per_row(jnp.cos(ang_r)), per_row(jnp.sin(ang_r))
    cos_c, sin_c = per_col(jnp.cos(ang_c)), per_col(jnp.sin(ang_c))
    cos = jnp.concatenate([cos_r, cos_r, cos_c, cos_c], axis=-1)
    sin = jnp.concatenate([-sin_r, sin_r, -sin_c, sin_c], axis=-1)
    return jnp.concatenate([cos, cos], axis=-1), jnp.concatenate([sin, sin], axis=-1)


def _pick_tile(n, target):
    t = min(n, target)
    while n % t:
        t //= 2
    return t


def kernel(x, c, ctx, c_ctx, w_mod, b_mod, norm1_g, norm2_g, w_in, lru_conv_w, lru_conv_b, lru_wa, lru_ba,
           lru_wx, lru_bx, lru_lambda, diff_lambda, diff_subln_g, na_rpb, w_out, ffn_w_gu, ffn_w_down, final_g):
    batch, seq, d = x.shape
    n_ctx = ctx.shape[1]
    depth = w_mod.shape[0]
    lru_w = lru_conv_w.shape[-1]
    diff_w = d // 2
    na_w = d - lru_w - diff_w
    sections = (lru_w, lru_w, diff_w, diff_w, diff_w, na_w, na_w, na_w)
    assert sum(sections) == w_in.shape[-1], (sections, w_in.shape)
    hidden = ffn_w_down.shape[1]

    cvec = jnp.concatenate([c, c_ctx[None], jnp.zeros((MOD_ROWS - batch - 1, d), F32)], axis=0)
    mods = _mod_call(cvec, w_mod, b_mod).reshape(depth * MOD_ROWS * 6, 1, d)
    rope_tabs = _rope_tables(seq)

    w_in_b = w_in.astype(BF16)
    w_out_b = w_out.astype(BF16)
    w_gu_b = ffn_w_gu.astype(BF16)
    w_down_b = ffn_w_down.astype(BF16)
    n1g = norm1_g.reshape(depth, 1, d)
    n2g = norm2_g.reshape(depth, 1, d)
    fg = final_g.reshape(1, d)
    eye = jnp.eye(LRU_BLOCKS, dtype=F32)[None, None, :, None, :, None]
    dense = lambda w: (eye * w[:, :, :, :, None, :]).reshape(depth, 2, lru_w, lru_w)
    w_gates = jnp.concatenate([dense(lru_wa), dense(lru_wx)], axis=-1).astype(BF16)
    b_gates = jnp.concatenate([lru_ba, lru_bx], axis=-1)

    xs = x.reshape(batch * seq, d)
    hs = ctx.reshape(batch * n_ctx, d)
    tm = _pick_tile(seq, 512)
    tm_c = _pick_tile(n_ctx, 256)
    tc = _pick_tile(seq, 512)
    tc_c = _pick_tile(n_ctx, 256)
    zeros_h = jnp.zeros((batch, lru_w), F32)
    h_chunk = -(-hidden // (2 * MXU_K)) * MXU_K
    lat = dict(shared_row=None, rows_per_batch=seq, tm=tm)
    con = dict(shared_row=batch, rows_per_batch=n_ctx, tm=tm_c)

    for l in range(depth):
        last = l == depth - 1
        lam_init = 0.8 - 0.6 * math.exp(-0.3 * l)
        px = _in_proj_call(xs, n1g, mods, w_in_b, rope_tabs, layer=l, sections=sections, **lat)
        pc = _in_proj_call(hs, n1g, mods, w_in_b, None, layer=l, sections=sections, **con)
        xl, gl, qd, kd, vd, qn, kn, vn = px
        cxl, cgl, cqd, ckd, cvd, cqn, ckn, cvn = pc

        lru_f = (lru_conv_w[l], lru_conv_b[l], w_gates[l, 0], b_gates[l, 0], lru_lambda[l, 0])
        lru_b = (lru_conv_w[l], lru_conv_b[l], w_gates[l, 1], b_gates[l, 1], lru_lambda[l, 1])
        hc_f = _lru_call(cxl, *lru_f, zeros_h, seq=n_ctx, reverse=False, tc=tc_c)
        hc_b, lru_c = _lru_call(cxl, *lru_b, zeros_h, hc_f, cgl, seq=n_ctx, reverse=True, tc=tc_c)
        hcf_last = hc_f.reshape(batch, n_ctx, lru_w)[:, -1]
        hcb_first = hc_b.reshape(batch, n_ctx, lru_w)[:, 0]
        hl_f = _lru_call(xl, *lru_f, hcf_last, seq=seq, reverse=False, tc=tc)
        _, lru_x = _lru_call(xl, *lru_b, hcb_first, hl_f, gl, seq=seq, reverse=True, tc=tc)

        ckd3 = ckd.reshape(batch, n_ctx, diff_w)
        cvd3 = cvd.reshape(batch, n_ctx, diff_w)
        diff_x = _diff_attn_call(qd, ckd3, cvd3, kd.reshape(batch, seq, diff_w), vd.reshape(batch, seq, diff_w),
                                 diff_lambda[l], diff_subln_g[l], seq=seq, tq=_pick_tile(seq, 256),
                                 tk=_pick_tile(seq // 4, 2048), lam_init=lam_init)

        bias = _rpb_table(na_rpb[l])
        ckn3 = ckn.reshape(batch, n_ctx, na_w)
        cvn3 = cvn.reshape(batch, n_ctx, na_w)
        na_x = _na_call(qn, kn.reshape(batch, seq, na_w), vn.reshape(batch, seq, na_w), ckn3, cvn3, bias, seq=seq,
                        row_block=8)

        xs = _out_ffn_call(xs, lru_x, diff_x, na_x, w_out_b, n2g, mods, w_gu_b, w_down_b, fg if last else None,
                           layer=l, h_chunk=h_chunk, **lat)
        if not last:
            diff_c = _diff_attn_call(cqd, ckd3, cvd3, None, None, diff_lambda[l], diff_subln_g[l],
                                     seq=n_ctx, tq=n_ctx, tk=n_ctx, lam_init=lam_init)
            na_c = _plain_attn_call(cqn, ckn, cvn, seq=n_ctx)
            hs = _out_ffn_call(hs, lru_c, diff_c, na_c, w_out_b, n2g, mods, w_gu_b, w_down_b, None,
                               layer=l, h_chunk=h_chunk, **con)
    return xs.reshape(batch, seq, d)
```

```python
import functools
import math

import jax
import jax.numpy as jnp
from jax import lax
from jax.experimental import pallas as pl
from jax.experimental.pallas import tpu as pltpu

GRID_W = 64
HEAD_DIM = 64
LRU_BLOCKS = 4
LRU_CONV = 4
LRU_C = 8.0
NA_KH = 8
NA_KW = 16
ROPE_BASE = 10000.0
NORM_EPS = 1e-6
NEG_INF = -1e30
LOG2E = math.log2(math.e)
MXU_K = 256
SUBLANES = 8
MOD_ROWS = 8
CONV_HALO = 8
VMEM_LIMIT = 56 * 1024 * 1024

F32 = jnp.float32
BF16 = jnp.bfloat16


def _cparams(sem):
    return pltpu.CompilerParams(dimension_semantics=sem, vmem_limit_bytes=VMEM_LIMIT)


def _full(shape):
    nd = len(shape)
    return pl.BlockSpec(shape, lambda *_: (0,) * nd, pipeline_mode=pl.Buffered(1))


def _layer(arr, l):
    tail = arr.shape[1:]
    return pl.BlockSpec((None,) + tail, lambda *_: (l,) + (0,) * len(tail), pipeline_mode=pl.Buffered(1))


def _mod_spec(mods, l, k, per_batch, shared_row):
    d = mods.shape[-1]
    base = (l * MOD_ROWS) * 6 + k
    if shared_row is not None:
        return pl.BlockSpec((None, 1, d), lambda i: (base + shared_row * 6, 0, 0))
    return pl.BlockSpec((None, 1, d), lambda i: (base + (i // per_batch) * 6, 0, 0))


def _rms(x, g):
    return x * lax.rsqrt(jnp.mean(x * x, axis=-1, keepdims=True) + NORM_EPS) * g


def _sigmoid(x):
    return 1.0 / (1.0 + jnp.exp(-x))


def _dot_t(a, b):
    return lax.dot_general(a, b, (((1,), (1,)), ((), ())), preferred_element_type=F32)


def _mod_kernel(c_ref, w_ref, b_ref, o_ref):
    cv = c_ref[...]
    cond = cv * _sigmoid(cv)
    o_ref[...] = jnp.dot(cond, w_ref[...], preferred_element_type=F32,
                         precision=lax.Precision.HIGHEST) + b_ref[...]


def _mod_call(cvec, w_mod, b_mod):
    depth, d, n = w_mod.shape
    rows = cvec.shape[0]
    tn = d
    return pl.pallas_call(
        _mod_kernel,
        grid=(depth, n // tn),
        in_specs=[pl.BlockSpec((rows, d), lambda l, j: (0, 0)),
                  pl.BlockSpec((None, d, tn), lambda l, j: (l, 0, j)),
                  pl.BlockSpec((None, 1, tn), lambda l, j: (l, 0, j))],
        out_specs=pl.BlockSpec((None, rows, tn), lambda l, j: (l, 0, j)),
        out_shape=jax.ShapeDtypeStruct((depth, rows, n), F32),
        compiler_params=_cparams(("arbitrary", "arbitrary")),
        name="mod",
    )(cvec, w_mod, b_mod.reshape(depth, 1, n))


def _rope(p, cos, sin_signed):
    n = p.shape[-1]
    lane = lax.broadcasted_iota(jnp.int32, p.shape, 1)
    low = (lane % 32) < 16
    partner = jnp.where(low, pltpu.roll(p, n - 16, 1), pltpu.roll(p, 16, 1))
    return p * cos + partner * sin_signed


def _in_proj_kernel(*refs, sections, rope, q_scale):
    if rope:
        x_ref, g_ref, sh_ref, sc_ref, w_ref, cos_ref, sin_ref = refs[:7]
        outs = refs[7:]
    else:
        x_ref, g_ref, sh_ref, sc_ref, w_ref = refs[:5]
        outs = refs[5:]
    h = _rms(x_ref[...], g_ref[...]) * (1.0 + sc_ref[...]) + sh_ref[...]
    hb = h.astype(BF16)
    off = 0
    for idx, (width, o_ref) in enumerate(zip(sections, outs)):
        p = jnp.dot(hb, w_ref[:, off:off + width], preferred_element_type=F32)
        if rope and idx in (2, 3):
            reps = width // cos_ref.shape[-1]
            cos = jnp.concatenate([cos_ref[...]] * reps, axis=-1)
            sin = jnp.concatenate([sin_ref[...]] * reps, axis=-1)
            p = _rope(p, cos, sin)
        if idx in (2, 5):
            p = p * q_scale
        o_ref[...] = p.astype(o_ref.dtype)
        off += width


def _in_proj_call(xs, g, mods, w, rope_tabs, *, layer, shared_row, sections, rows_per_batch, tm):
    t, d = xs.shape
    per_batch = rows_per_batch // tm
    in_specs = [pl.BlockSpec((tm, d), lambda i: (i, 0)),
                _layer(g, layer),
                _mod_spec(mods, layer, 0, per_batch, shared_row),
                _mod_spec(mods, layer, 1, per_batch, shared_row),
                _layer(w, layer)]
    args = [xs, g, mods, mods, w]
    rope = rope_tabs is not None
    if rope:
        cos, sin = rope_tabs
        tw = cos.shape[-1]
        in_specs += [pl.BlockSpec((tm, tw), lambda i: (i % per_batch, 0))] * 2
        args += [cos, sin]
    dtypes = (F32, F32, BF16, BF16, BF16, BF16, BF16, BF16)
    out_shape = tuple(jax.ShapeDtypeStruct((t, wd), dt) for wd, dt in zip(sections, dtypes))
    out_specs = tuple(pl.BlockSpec((tm, wd), lambda i: (i, 0)) for wd in sections)
    return pl.pallas_call(
        functools.partial(_in_proj_kernel, sections=sections, rope=rope, q_scale=HEAD_DIM ** -0.5 * LOG2E),
        grid=(t // tm,),
        in_specs=in_specs,
        out_specs=out_specs,
        out_shape=out_shape,
        compiler_params=_cparams(("arbitrary",)),
        name="in_proj_rope" if rope else "in_proj",
    )(*args)


def _lru_kernel(*refs, reverse, combine, tc, n_chunks):
    if combine:
        (prev_ref, x_ref, next_ref, cw_ref, cb_ref, w_ref, bias_ref, lam_ref, h0_ref,
         hf_ref, gate_ref, h_out_ref, y_out_ref, carry_ref) = refs
    else:
        (prev_ref, x_ref, next_ref, cw_ref, cb_ref, w_ref, bias_ref, lam_ref, h0_ref,
         h_out_ref, carry_ref) = refs
    j = pl.program_id(1)
    chunk = (n_chunks - 1 - j) if reverse else j

    @pl.when(j == 0)
    def _():
        carry_ref[...] = h0_ref[...]

    width = x_ref.shape[-1]
    has_prev = (chunk > 0).astype(F32)
    has_next = (chunk < n_chunks - 1).astype(F32)
    xcat = jnp.concatenate([prev_ref[...] * has_prev, x_ref[...], next_ref[...] * has_next], axis=0)
    left = LRU_CONV // 2
    u = cb_ref[...]
    for tap in range(LRU_CONV):
        start = CONV_HALO - left + tap
        u = u + cw_ref[tap:tap + 1, :] * xcat[start:start + tc, :]

    gates = jnp.dot(u.astype(BF16), w_ref[...], preferred_element_type=F32) + bias_ref[...]
    r = _sigmoid(gates[:, :width])
    i = _sigmoid(gates[:, width:])
    lam = lam_ref[...]
    log_sig = jnp.minimum(lam, 0.0) - jnp.log(1.0 + jnp.exp(-jnp.abs(lam)))
    log_a = LRU_C * r * log_sig
    a = jnp.exp(log_a)
    b = jnp.sqrt(1.0 - jnp.exp(2.0 * log_a)) * (i * u)

    n_tiles = tc // SUBLANES
    a = a.reshape(n_tiles, SUBLANES, width)
    b = b.reshape(n_tiles, SUBLANES, width)
    row = lax.broadcasted_iota(jnp.int32, a.shape, 1)
    d = 1
    while d < SUBLANES:
        if reverse:
            valid = row < SUBLANES - d
            shift = SUBLANES - d
        else:
            valid = row >= d
            shift = d
        a_sh = jnp.where(valid, pltpu.roll(a, shift, 1), 1.0)
        b_sh = jnp.where(valid, pltpu.roll(b, shift, 1), 0.0)
        b = a * b_sh + b
        a = a * a_sh
        d *= 2
    a = a.reshape(tc, width)
    b = b.reshape(tc, width)
    state = carry_ref[...]
    tiles = [None] * n_tiles
    for t in (range(n_tiles - 1, -1, -1) if reverse else range(n_tiles)):
        lo = t * SUBLANES
        tiles[t] = b[lo:lo + SUBLANES, :] + a[lo:lo + SUBLANES, :] * state
        state = tiles[t][0:1, :] if reverse else tiles[t][SUBLANES - 1:SUBLANES, :]
    carry_ref[...] = state
    h = jnp.concatenate(tiles, axis=0)
    h_out_ref[...] = h
    if combine:
        g = gate_ref[...]
        gelu = 0.5 * g * (1.0 + jnp.tanh(math.sqrt(2.0 / math.pi) * (g + 0.044715 * (g * g * g))))
        y_out_ref[...] = ((hf_ref[...] + h) * gelu).astype(y_out_ref.dtype)


def _lru_call(xl, conv_w, conv_b, w_gates, b_gates, lam, h0, hf=None, gate=None, *,
              seq, reverse, tc):
    t, width = xl.shape
    batch = t // seq
    n_chunks = seq // tc
    hb = tc // CONV_HALO
    n_halo = seq // CONV_HALO
    combine = hf is not None

    def chunk_of(j):
        return (n_chunks - 1 - j) if reverse else j

    def main_idx(b, j):
        return (b * n_chunks + chunk_of(j), 0)

    def prev_idx(b, j):
        return (b * n_halo + jnp.maximum(chunk_of(j) * hb - 1, 0), 0)

    def next_idx(b, j):
        return (b * n_halo + jnp.minimum((chunk_of(j) + 1) * hb, n_halo - 1), 0)

    in_specs = [pl.BlockSpec((CONV_HALO, width), prev_idx),
                pl.BlockSpec((tc, width), main_idx),
                pl.BlockSpec((CONV_HALO, width), next_idx),
                _full(conv_w.shape), _full((1, width)), _full(w_gates.shape),
                _full((1, 2 * width)), _full((1, width)),
                pl.BlockSpec((None, 1, width), lambda b, j: (b, 0, 0))]
    args = [xl, xl, xl, conv_w, conv_b.reshape(1, width), w_gates, b_gates.reshape(1, 2 * width),
            lam.reshape(1, width), h0.reshape(batch, 1, width)]
    out_shape = [jax.ShapeDtypeStruct((t, width), F32)]
    out_specs = [pl.BlockSpec((tc, width), main_idx)]
    if combine:
        in_specs += [pl.BlockSpec((tc, width), main_idx)] * 2
        args += [hf, gate]
        out_shape.append(jax.ShapeDtypeStruct((t, width), BF16))
        out_specs.append(pl.BlockSpec((tc, width), main_idx))
    res = pl.pallas_call(
        functools.partial(_lru_kernel, reverse=reverse, combine=combine, tc=tc, n_chunks=n_chunks),
        grid=(batch, n_chunks),
        in_specs=in_specs,
        out_specs=tuple(out_specs),
        out_shape=tuple(out_shape),
        scratch_shapes=[pltpu.VMEM((1, width), F32)],
        compiler_params=_cparams(("arbitrary", "arbitrary")),
        name="lru_bwd" if reverse else "lru_fwd",
    )(*args)
    return res if combine else res[0]


def _diff_lambda(dl_ref, lam_init):
    dl = dl_ref[...]
    s01 = jnp.sum(dl[0:1, :] * dl[1:2, :], axis=-1, keepdims=True)
    s23 = jnp.sum(dl[2:3, :] * dl[3:4, :], axis=-1, keepdims=True)
    return jnp.exp(s01) - jnp.exp(s23) + lam_init


def _diff_attn_kernel(q_ref, *refs, tq, n_blk, tk, n_kt, lam_init):
    if n_kt:
        o_ref = refs[-3]
        rest = refs[:-3] + refs[-2:]
    else:
        o_ref = refs[-1]
        rest = refs[:-1]
    for blk in range(n_blk):
        rows = slice(blk * tq, (blk + 1) * tq)
        o_ref[rows, :] = _diff_attn_block(q_ref[rows, :], *rest, tq=tq, tk=tk, n_kt=n_kt,
                                          lam_init=lam_init).astype(o_ref.dtype)


def _diff_attn_block(q, *refs, tq, tk, n_kt, lam_init):
    if n_kt:
        kc_ref, vc_ref, k_ref, v_ref, dl_ref, g_ref, s_a, s_b = refs
    else:
        kc_ref, vc_ref, dl_ref, g_ref = refs
    lane = lax.broadcasted_iota(jnp.int32, q.shape, 1)
    zero = jnp.zeros_like(q)
    qs = jnp.concatenate([jnp.where(lane < HEAD_DIM, q, zero), jnp.where(lane >= HEAD_DIM, q, zero)], axis=0)
    vw = vc_ref.shape[-1]

    def scores(kt):
        s = _dot_t(qs, kt)
        return s, jnp.max(s, axis=-1, keepdims=True)

    def update(load_s, vt, m_tile, m, acc):
        m_new = jnp.maximum(m, m_tile)
        acc = jnp.exp2(m - m_new) * acc
        v_ext = jnp.concatenate([vt, jnp.ones_like(vt)], axis=1)
        n_keys = vt.shape[0]
        kc = min(n_keys, MXU_K)
        for c0 in range(0, n_keys, kc):
            p = jnp.exp2(load_s(c0, kc) - m_new).astype(BF16)
            acc = acc + jnp.dot(p, v_ext[c0:c0 + kc, :], preferred_element_type=F32)
        return m_new, acc

    def k_tile(j):
        return k_ref[pl.ds(pl.multiple_of(j * tk, tk), tk), :]

    def v_tile(j):
        return v_ref[pl.ds(pl.multiple_of(j * tk, tk), tk), :]

    m = jnp.full((2 * tq, 1), -jnp.inf, F32)
    acc = jnp.zeros((2 * tq, 2 * vw), F32)
    if n_kt:
        s_first, mt_a = scores(k_tile(0))
        s_a[...] = s_first

        def pair(j, carry, last):
            m, acc, mt_a = carry
            s_next, mt_b = scores(k_tile(j + 1))
            s_b[...] = s_next
            m, acc = update(lambda c0, n: s_a[:, c0:c0 + n], v_tile(j), mt_a, m, acc)
            if last:
                s_ctx, mt_a = scores(kc_ref[...])
            else:
                s_ctx = None
                s_next, mt_a = scores(k_tile(j + 2))
                s_a[...] = s_next
            m, acc = update(lambda c0, n: s_b[:, c0:c0 + n], v_tile(j + 1), mt_b, m, acc)
            return (m, acc, mt_a), s_ctx

        carry = lax.fori_loop(0, n_kt // 2 - 1, lambda t, c: pair(2 * t, c, False)[0], (m, acc, mt_a))
        (m, acc, mt_c), s_c = pair(n_kt - 2, carry, True)
    else:
        s_c, mt_c = scores(kc_ref[...])
    m, acc = update(lambda c0, n: s_c[:, c0:c0 + n], vc_ref[...], mt_c, m, acc)
    o = acc[:, :vw] / acc[:, vw:vw + 1]
    lam = _diff_lambda(dl_ref, lam_init)
    od = o[:tq, :] - lam * o[tq:, :]
    return _rms(od, g_ref[...]) * (1.0 - lam_init)


def _diff_attn_call(q, kc, vc, k, v, dl, g, *, seq, n_ctx, tq, n_blk, tk, lam_init):
    t, width = q.shape
    batch = t // seq
    hw = 2 * HEAD_DIM
    heads = width // hw
    nq = seq // (tq * n_blk)
    kv_spec = lambda n: pl.BlockSpec((n, hw), lambda b, h, i: (b, h))
    q_spec = pl.BlockSpec((tq * n_blk, hw), lambda b, h, i: (b * nq + i, h))
    in_specs = [q_spec, kv_spec(n_ctx), kv_spec(n_ctx)]
    args = [q, kc, vc]
    n_kt = 0
    if k is not None:
        n_lat = k.shape[0] // batch
        n_kt = n_lat // tk
        in_specs += [kv_spec(n_lat), kv_spec(n_lat)]
        args += [k, v]
    in_specs += [_full(dl.shape), _full((1, hw))]
    args += [dl, g.reshape(1, hw)]
    assert n_kt % 2 == 0, n_kt
    scratch = [pltpu.VMEM((2 * tq, tk), F32)] * 2 if n_kt else []
    return pl.pallas_call(
        functools.partial(_diff_attn_kernel, tq=tq, n_blk=n_blk, tk=tk, n_kt=n_kt, lam_init=lam_init),
        grid=(batch, heads, nq),
        in_specs=in_specs,
        out_specs=q_spec,
        out_shape=jax.ShapeDtypeStruct((t, width), BF16),
        scratch_shapes=scratch,
        compiler_params=_cparams(("arbitrary", "arbitrary", "arbitrary")),
        name="diff_attn" if n_kt else "diff_attn_ctx",
    )(*args)


def _rpb_table_kernel(r_ref, o_ref):
    n = o_ref.shape[-1]
    j = lax.broadcasted_iota(jnp.int32, (r_ref.shape[-1], n), 0)
    pos = lax.broadcasted_iota(jnp.int32, (r_ref.shape[-1], n), 1)
    qc = pos // GRID_W
    kc = pos % GRID_W
    coff = jnp.clip(kc - qc + (NA_KW - 1), 0, 2 * NA_KW - 2)
    onehot = (coff == j).astype(F32)
    vals = jnp.dot(r_ref[...], onehot, preferred_element_type=F32, precision=lax.Precision.HIGHEST)
    pos1 = lax.broadcasted_iota(jnp.int32, vals.shape, 1)
    qc1 = pos1 // GRID_W
    kc1 = pos1 % GRID_W
    c0 = jnp.clip(qc1 - NA_KW // 2, 0, GRID_W - NA_KW)
    in_win = (kc1 >= c0) & (kc1 < c0 + NA_KW)
    o_ref[...] = jnp.where(in_win, vals * LOG2E, NEG_INF)


def _rpb_table(rpb):
    heads = rpb.shape[0]
    ncol = 2 * NA_KW
    rp = jnp.pad(rpb, ((0, 0), (0, 0), (0, ncol - rpb.shape[-1])))
    r_all = jnp.stack([rp[:, NA_KH - 1 - p:2 * NA_KH - 1 - p, :] for p in range(NA_KH)], axis=0)
    r_all = r_all.reshape(NA_KH, heads * NA_KH, ncol)
    n = GRID_W * GRID_W
    tab = pl.pallas_call(
        _rpb_table_kernel,
        grid=(NA_KH,),
        in_specs=[pl.BlockSpec((None, heads * NA_KH, ncol), lambda p: (p, 0, 0))],
        out_specs=pl.BlockSpec((None, heads * NA_KH, n), lambda p: (p, 0, 0)),
        out_shape=jax.ShapeDtypeStruct((NA_KH, heads * NA_KH, n), F32),
        compiler_params=_cparams(("arbitrary",)),
        name="rpb_table",
    )(r_all)
    tab = tab.reshape(NA_KH, heads, NA_KH, GRID_W, GRID_W).transpose(0, 1, 3, 2, 4)
    return tab.reshape(NA_KH, heads * GRID_W, NA_KH * GRID_W)


def _stack_heads(q, heads):
    head_of_lane = lax.broadcasted_iota(jnp.int32, q.shape, 1) // HEAD_DIM
    zero = jnp.zeros_like(q)
    return jnp.concatenate([jnp.where(head_of_lane == h, q, zero) for h in range(heads)], axis=0)


def _unstack_heads(res, heads):
    n = res.shape[0] // heads
    head_of_lane = lax.broadcasted_iota(jnp.int32, (n, res.shape[1]), 1) // HEAD_DIM
    out = jnp.zeros((n, res.shape[1]), res.dtype)
    for h in range(heads):
        out = jnp.where(head_of_lane == h, res[h * n:(h + 1) * n, :], out)
    return out


def _na_kernel(q_ref, k_ref, v_ref, kc_ref, vc_ref, bias_ref, o_ref, *, rows, heads, row_block):
    n_loc = NA_KH * GRID_W
    for i in range(row_block):
        r = pl.program_id(1) * row_block + i
        r0 = jnp.clip(r - NA_KH // 2, 0, rows - NA_KH)
        pat = r - r0
        start = pl.multiple_of(r0 * GRID_W, GRID_W)
        ks = k_ref[pl.ds(start, n_loc), :]
        vs = v_ref[pl.ds(start, n_loc), :]
        qs = _stack_heads(q_ref[i * GRID_W:(i + 1) * GRID_W, :], heads)
        s_loc = _dot_t(qs, ks) + bias_ref[pat]
        s_ctx = _dot_t(qs, kc_ref[...])
        m = jnp.maximum(jnp.max(s_loc, axis=-1, keepdims=True), jnp.max(s_ctx, axis=-1, keepdims=True))
        e_loc = jnp.exp2(s_loc - m)
        e_ctx = jnp.exp2(s_ctx - m)
        denom = jnp.sum(e_loc, axis=-1, keepdims=True) + jnp.sum(e_ctx, axis=-1, keepdims=True)
        res = (jnp.dot(e_loc.astype(BF16), vs, preferred_element_type=F32)
               + jnp.dot(e_ctx.astype(BF16), vc_ref[...], preferred_element_type=F32)) / denom
        o_ref[i * GRID_W:(i + 1) * GRID_W, :] = _unstack_heads(res, heads).astype(o_ref.dtype)


def _na_call(q, k, v, kc, vc, bias, *, seq, n_ctx, row_block):
    t, width = q.shape
    batch = t // seq
    rows = seq // GRID_W
    assert rows >= NA_KH and rows % row_block == 0, (rows, row_block)
    heads = width // HEAD_DIM
    steps = rows // row_block
    tq = row_block * GRID_W
    per_batch = lambda n: pl.BlockSpec((n, width), lambda b, r: (b, 0))
    return pl.pallas_call(
        functools.partial(_na_kernel, rows=rows, heads=heads, row_block=row_block),
        grid=(batch, steps),
        in_specs=[pl.BlockSpec((tq, width), lambda b, r: (b * steps + r, 0)),
                  per_batch(seq), per_batch(seq), per_batch(n_ctx), per_batch(n_ctx),
                  _full(bias.shape)],
        out_specs=pl.BlockSpec((tq, width), lambda b, r: (b * steps + r, 0)),
        out_shape=jax.ShapeDtypeStruct((t, width), BF16),
        compiler_params=_cparams(("arbitrary", "arbitrary")),
        name="na_attn",
    )(q, k, v, kc, vc, bias)


def _plain_attn_kernel(q_ref, k_ref, v_ref, o_ref, *, heads):
    qs = _stack_heads(q_ref[...], heads)
    s = _dot_t(qs, k_ref[...])
    e = jnp.exp2(s - jnp.max(s, axis=-1, keepdims=True))
    res = jnp.dot(e.astype(BF16), v_ref[...], preferred_element_type=F32) / jnp.sum(e, axis=-1, keepdims=True)
    o_ref[...] = _unstack_heads(res, heads).astype(o_ref.dtype)


def _plain_attn_call(q, k, v, *, seq):
    t, width = q.shape
    spec = pl.BlockSpec((seq, width), lambda b: (b, 0))
    return pl.pallas_call(
        functools.partial(_plain_attn_kernel, heads=width // HEAD_DIM),
        grid=(t // seq,),
        in_specs=[spec, spec, spec],
        out_specs=spec,
        out_shape=jax.ShapeDtypeStruct((t, width), BF16),
        compiler_params=_cparams(("arbitrary",)),
        name="ctx_plain_attn",
    )(q, k, v)


def _out_ffn_kernel(*refs, widths, hidden, h_chunk, final):
    (x_ref, lru_ref, diff_ref, na_ref, wo_ref, g1_ref, n2_ref, sh_ref, sc_ref, g2_ref,
     wgu_ref, wd_ref) = refs[:12]
    if final:
        fg_ref, o_ref = refs[12:]
    else:
        o_ref = refs[12]
    off = 0
    mix = None
    for part_ref, wd_ in zip((lru_ref, diff_ref, na_ref), widths):
        term = jnp.dot(part_ref[...], wo_ref[off:off + wd_, :], preferred_element_type=F32)
        mix = term if mix is None else mix + term
        off += wd_
    x1 = x_ref[...] + g1_ref[...] * mix
    hb = (_rms(x1, n2_ref[...]) * (1.0 + sc_ref[...]) + sh_ref[...]).astype(BF16)
    acc = None
    for c0 in range(0, hidden, h_chunk):
        c1 = min(c0 + h_chunk, hidden)
        g = jnp.dot(hb, wgu_ref[:, c0:c1], preferred_element_type=F32)
        u = jnp.dot(hb, wgu_ref[:, hidden + c0:hidden + c1], preferred_element_type=F32)
        a = (g * _sigmoid(g) * u).astype(BF16)
        term = jnp.dot(a, wd_ref[c0:c1, :], preferred_element_type=F32)
        acc = term if acc is None else acc + term
    x2 = x1 + g2_ref[...] * acc
    if final:
        x2 = _rms(x2, fg_ref[...])
    o_ref[...] = x2


def _out_ffn_call(xs, lru, diff, na, w_out, n2g, mods, w_gu, w_down, final_g,
                  *, layer, shared_row, rows_per_batch, tm, h_chunk):
    t, d = xs.shape
    hidden = w_down.shape[1]
    per_batch = rows_per_batch // tm
    mod = lambda k: _mod_spec(mods, layer, k, per_batch, shared_row)
    row = lambda wd_: pl.BlockSpec((tm, wd_), lambda i: (i, 0))
    widths = (lru.shape[1], diff.shape[1], na.shape[1])
    in_specs = [row(d), row(widths[0]), row(widths[1]), row(widths[2]), _layer(w_out, layer), mod(2),
                _layer(n2g, layer), mod(3), mod(4), mod(5), _layer(w_gu, layer), _layer(w_down, layer)]
    args = [xs, lru, diff, na, w_out, mods, n2g, mods, mods, mods, w_gu, w_down]
    final = final_g is not None
    if final:
        in_specs.append(_full((1, d)))
        args.append(final_g)
    return pl.pallas_call(
        functools.partial(_out_ffn_kernel, widths=widths, hidden=hidden, h_chunk=h_chunk, final=final),
        grid=(t // tm,),
        in_specs=in_specs,
        out_specs=row(d),
        out_shape=jax.ShapeDtypeStruct((t, d), F32),
        compiler_params=_cparams(("arbitrary",)),
        name="out_ffn_final" if final else "out_ffn",
    )(*args)


def _rope_tables(seq):
    rows = seq // GRID_W
    axis_dim = HEAD_DIM // 2
    inv_freq = 1.0 / (ROPE_BASE ** (jnp.arange(0, axis_dim, 2, dtype=F32) / axis_dim))
    ang_r = jnp.arange(rows, dtype=F32)[:, None] * inv_freq[None]
    ang_c = jnp.arange(GRID_W, dtype=F32)[:, None] * inv_freq[None]
    per_row = lambda a: jnp.broadcast_to(a[:, None, :], (rows, GRID_W, a.shape[-1])).reshape(seq, -1)
    per_col = lambda a: jnp.broadcast_to(a[None, :, :], (rows, GRID_W, a.shape[-1])).reshape(seq, -1)
    cos_r, sin_r = per_row(jnp.cos(ang_r)), per_row(jnp.sin(ang_r))
    cos_c, sin_c = per_col(jnp.cos(ang_c)), per_col(jnp.sin(ang_c))
    cos = jnp.concatenate([cos_r, cos_r, cos_c, cos_c], axis=-1)
    sin = jnp.concatenate([-sin_r, sin_r, -sin_c, sin_c], axis=-1)
    return jnp.concatenate([cos, cos], axis=-1), jnp.concatenate([sin, sin], axis=-1)


def _pick_tile(n, target):
    t = min(n, target)
    while n % t:
        t //= 2
    return t


def kernel(x, c, ctx, c_ctx, w_mod, b_mod, norm1_g, norm2_g, w_in, lru_conv_w, lru_conv_b, lru_wa, lru_ba,
           lru_wx, lru_bx, lru_lambda, diff_lambda, diff_subln_g, na_rpb, w_out, ffn_w_gu, ffn_w_down, final_g):
    batch, seq, d = x.shape
    n_ctx = ctx.shape[1]
    depth = w_mod.shape[0]
    lru_w = lru_conv_w.shape[-1]
    diff_w = d // 2
    na_w = d - lru_w - diff_w
    sections = (lru_w, lru_w, diff_w, diff_w, diff_w, na_w, na_w, na_w)
    assert sum(sections) == w_in.shape[-1], (sections, w_in.shape)
    hidden = ffn_w_down.shape[1]

    cvec = jnp.concatenate([c, c_ctx[None], jnp.zeros((MOD_ROWS - batch - 1, d), F32)], axis=0)
    mods = _mod_call(cvec, w_mod, b_mod).reshape(depth * MOD_ROWS * 6, 1, d)
    rope_tabs = _rope_tables(seq)

    w_in_b = w_in.astype(BF16)
    w_out_b = w_out.astype(BF16)
    w_gu_b = ffn_w_gu.astype(BF16)
    w_down_b = ffn_w_down.astype(BF16)
    n1g = norm1_g.reshape(depth, 1, d)
    n2g = norm2_g.reshape(depth, 1, d)
    fg = final_g.reshape(1, d)
    eye = jnp.eye(LRU_BLOCKS, dtype=F32)[None, None, :, None, :, None]
    dense = lambda w: (eye * w[:, :, :, :, None, :]).reshape(depth, 2, lru_w, lru_w)
    w_gates = jnp.concatenate([dense(lru_wa), dense(lru_wx)], axis=-1).astype(BF16)
    b_gates = jnp.concatenate([lru_ba, lru_bx], axis=-1)

    xs = x.reshape(batch * seq, d)
    hs = ctx.reshape(batch * n_ctx, d)
    tm = _pick_tile(seq, 512)
    tm_c = _pick_tile(n_ctx, 256)
    tc = _pick_tile(seq, 512)
    tc_c = _pick_tile(n_ctx, 256)
    zeros_h = jnp.zeros((batch, lru_w), F32)
    h_chunk = -(-hidden // (2 * MXU_K)) * MXU_K
    lat = dict(shared_row=None, rows_per_batch=seq, tm=tm)
    con = dict(shared_row=batch, rows_per_batch=n_ctx, tm=tm_c)

    for l in range(depth):
        last = l == depth - 1
        lam_init = 0.8 - 0.6 * math.exp(-0.3 * l)
        px = _in_proj_call(xs, n1g, mods, w_in_b, rope_tabs, layer=l, sections=sections, **lat)
        pc = _in_proj_call(hs, n1g, mods, w_in_b, None, layer=l, sections=sections, **con)
        xl, gl, qd, kd, vd, qn, kn, vn = px
        cxl, cgl, cqd, ckd, cvd, cqn, ckn, cvn = pc

        lru_f = (lru_conv_w[l], lru_conv_b[l], w_gates[l, 0], b_gates[l, 0], lru_lambda[l, 0])
        lru_b = (lru_conv_w[l], lru_conv_b[l], w_gates[l, 1], b_gates[l, 1], lru_lambda[l, 1])
        hc_f = _lru_call(cxl, *lru_f, zeros_h, seq=n_ctx, reverse=False, tc=tc_c)
        hc_b, lru_c = _lru_call(cxl, *lru_b, zeros_h, hc_f, cgl, seq=n_ctx, reverse=True, tc=tc_c)
        hcf_last = hc_f.reshape(batch, n_ctx, lru_w)[:, -1]
        hcb_first = hc_b.reshape(batch, n_ctx, lru_w)[:, 0]
        hl_f = _lru_call(xl, *lru_f, hcf_last, seq=seq, reverse=False, tc=tc)
        _, lru_x = _lru_call(xl, *lru_b, hcb_first, hl_f, gl, seq=seq, reverse=True, tc=tc)

        tq = _pick_tile(seq, 256)
        diff_x = _diff_attn_call(qd, ckd, cvd, kd, vd, diff_lambda[l], diff_subln_g[l], seq=seq, n_ctx=n_ctx,
                                 tq=tq, n_blk=min(4, seq // tq), tk=_pick_tile(seq // 4, 2048), lam_init=lam_init)

        bias = _rpb_table(na_rpb[l])
        na_x = _na_call(qn, kn, vn, ckn, cvn, bias, seq=seq, n_ctx=n_ctx, row_block=8)

        xs = _out_ffn_call(xs, lru_x, diff_x, na_x, w_out_b, n2g, mods, w_gu_b, w_down_b, fg if last else None,
                           layer=l, h_chunk=h_chunk, **lat)
        if not last:
            diff_c = _diff_attn_call(cqd, ckd, cvd, None, None, diff_lambda[l], diff_subln_g[l],
                                     seq=n_ctx, n_ctx=n_ctx, tq=n_ctx, n_blk=1, tk=n_ctx, lam_init=lam_init)
            na_c = _plain_attn_call(cqn, ckn, cvn, seq=n_ctx)
            hs = _out_ffn_call(hs, lru_c, diff_c, na_c, w_out_b, n2g, mods, w_gu_b, w_down_b, None,
                               layer=l, h_chunk=h_chunk, **con)
    return xs.reshape(batch, seq, d)
```

```python
import functools
import math

import jax
import jax.numpy as jnp
from jax import lax
from jax.experimental import pallas as pl
from jax.experimental.pallas import tpu as pltpu

GRID_W = 64
HEAD_DIM = 64
LRU_BLOCKS = 4
LRU_CONV = 4
LRU_C = 8.0
NA_KH = 8
NA_KW = 16
ROPE_BASE = 10000.0
NORM_EPS = 1e-6
NEG_INF = -1e30
LOG2E = math.log2(math.e)
MXU_K = 256
SUBLANES = 8
MOD_ROWS = 8
CONV_HALO = 8
VMEM_LIMIT = 56 * 1024 * 1024

F32 = jnp.float32
BF16 = jnp.bfloat16


def _cparams(sem):
    return pltpu.CompilerParams(dimension_semantics=sem, vmem_limit_bytes=VMEM_LIMIT)


def _full(shape):
    nd = len(shape)
    return pl.BlockSpec(shape, lambda *_: (0,) * nd, pipeline_mode=pl.Buffered(1))


def _layer(arr, l):
    tail = arr.shape[1:]
    return pl.BlockSpec((None,) + tail, lambda *_: (l,) + (0,) * len(tail), pipeline_mode=pl.Buffered(1))


def _mod_spec(mods, l, k, per_batch, shared_row):
    d = mods.shape[-1]
    base = (l * MOD_ROWS) * 6 + k
    if shared_row is not None:
        return pl.BlockSpec((None, 1, d), lambda i: (base + shared_row * 6, 0, 0))
    return pl.BlockSpec((None, 1, d), lambda i: (base + (i // per_batch) * 6, 0, 0))


def _rms(x, g):
    return x * lax.rsqrt(jnp.mean(x * x, axis=-1, keepdims=True) + NORM_EPS) * g


def _sigmoid(x):
    return 1.0 / (1.0 + jnp.exp(-x))


def _dot_t(a, b):
    return lax.dot_general(a, b, (((1,), (1,)), ((), ())), preferred_element_type=F32)


def _mod_kernel(c_ref, w_ref, b_ref, o_ref):
    cv = c_ref[...]
    cond = cv * _sigmoid(cv)
    o_ref[...] = jnp.dot(cond, w_ref[...], preferred_element_type=F32,
                         precision=lax.Precision.HIGHEST) + b_ref[...]


def _mod_call(cvec, w_mod, b_mod):
    depth, d, n = w_mod.shape
    rows = cvec.shape[0]
    tn = d
    return pl.pallas_call(
        _mod_kernel,
        grid=(depth, n // tn),
        in_specs=[pl.BlockSpec((rows, d), lambda l, j: (0, 0)),
                  pl.BlockSpec((None, d, tn), lambda l, j: (l, 0, j)),
                  pl.BlockSpec((None, 1, tn), lambda l, j: (l, 0, j))],
        out_specs=pl.BlockSpec((None, rows, tn), lambda l, j: (l, 0, j)),
        out_shape=jax.ShapeDtypeStruct((depth, rows, n), F32),
        compiler_params=_cparams(("arbitrary", "arbitrary")),
        name="mod",
    )(cvec, w_mod, b_mod.reshape(depth, 1, n))


def _rope(p, cos, sin_signed):
    n = p.shape[-1]
    lane = lax.broadcasted_iota(jnp.int32, p.shape, 1)
    low = (lane % 32) < 16
    partner = jnp.where(low, pltpu.roll(p, n - 16, 1), pltpu.roll(p, 16, 1))
    return p * cos + partner * sin_signed


def _in_proj_kernel(*refs, sections, rope, q_scale, sub_rows):
    if rope:
        x_ref, g_ref, sh_ref, sc_ref, w_ref, cos_ref, sin_ref = refs[:7]
        outs = refs[7:]
    else:
        x_ref, g_ref, sh_ref, sc_ref, w_ref = refs[:5]
        outs = refs[5:]
    for r0 in range(0, x_ref.shape[0], sub_rows):
        rows = slice(r0, r0 + sub_rows)
        h = _rms(x_ref[rows, :], g_ref[...]) * (1.0 + sc_ref[...]) + sh_ref[...]
        hb = h.astype(BF16)
        off = 0
        for idx, (width, o_ref) in enumerate(zip(sections, outs)):
            p = jnp.dot(hb, w_ref[:, off:off + width], preferred_element_type=F32)
            if rope and idx in (2, 3):
                reps = width // cos_ref.shape[-1]
                cos = jnp.concatenate([cos_ref[rows, :]] * reps, axis=-1)
                sin = jnp.concatenate([sin_ref[rows, :]] * reps, axis=-1)
                p = _rope(p, cos, sin)
            if idx in (2, 5):
                p = p * q_scale
            o_ref[rows, :] = p.astype(o_ref.dtype)
            off += width


def _in_proj_call(xs, g, mods, w, rope_tabs, *, layer, shared_row, sections, rows_per_batch, tm):
    t, d = xs.shape
    per_batch = rows_per_batch // tm
    in_specs = [pl.BlockSpec((tm, d), lambda i: (i, 0)),
                _layer(g, layer),
                _mod_spec(mods, layer, 0, per_batch, shared_row),
                _mod_spec(mods, layer, 1, per_batch, shared_row),
                _layer(w, layer)]
    args = [xs, g, mods, mods, w]
    rope = rope_tabs is not None
    if rope:
        cos, sin = rope_tabs
        tw = cos.shape[-1]
        in_specs += [pl.BlockSpec((tm, tw), lambda i: (i % per_batch, 0))] * 2
        args += [cos, sin]
    dtypes = (F32, F32, BF16, BF16, BF16, BF16, BF16, BF16)
    out_shape = tuple(jax.ShapeDtypeStruct((t, wd), dt) for wd, dt in zip(sections, dtypes))
    out_specs = tuple(pl.BlockSpec((tm, wd), lambda i: (i, 0)) for wd in sections)
    return pl.pallas_call(
        functools.partial(_in_proj_kernel, sections=sections, rope=rope, q_scale=HEAD_DIM ** -0.5 * LOG2E,
                          sub_rows=min(tm, 512)),
        grid=(t // tm,),
        in_specs=in_specs,
        out_specs=out_specs,
        out_shape=out_shape,
        compiler_params=_cparams(("arbitrary",)),
        name="in_proj_rope" if rope else "in_proj",
    )(*args)


def _lru_kernel(*refs, reverse, combine, tc, n_chunks):
    if combine:
        (prev_ref, x_ref, next_ref, cw_ref, cb_ref, w_ref, bias_ref, lam_ref, h0_ref,
         hf_ref, gate_ref, h_out_ref, y_out_ref, carry_ref) = refs
    else:
        (prev_ref, x_ref, next_ref, cw_ref, cb_ref, w_ref, bias_ref, lam_ref, h0_ref,
         h_out_ref, carry_ref) = refs
    j = pl.program_id(1)
    chunk = (n_chunks - 1 - j) if reverse else j

    @pl.when(j == 0)
    def _():
        carry_ref[...] = h0_ref[...]

    width = x_ref.shape[-1]
    has_prev = (chunk > 0).astype(F32)
    has_next = (chunk < n_chunks - 1).astype(F32)
    xcat = jnp.concatenate([prev_ref[...] * has_prev, x_ref[...], next_ref[...] * has_next], axis=0)
    left = LRU_CONV // 2
    u = cb_ref[...]
    for tap in range(LRU_CONV):
        start = CONV_HALO - left + tap
        u = u + cw_ref[tap:tap + 1, :] * xcat[start:start + tc, :]

    gates = jnp.dot(u.astype(BF16), w_ref[...], preferred_element_type=F32) + bias_ref[...]
    r = 0.5 + 0.5 * jnp.tanh(0.5 * gates[:, :width])
    i = 0.5 + 0.5 * jnp.tanh(0.5 * gates[:, width:])
    lam = lam_ref[...]
    log_sig = jnp.minimum(lam, 0.0) - jnp.log(1.0 + jnp.exp(-jnp.abs(lam)))
    log_a = LRU_C * r * log_sig
    a = jnp.exp(log_a)
    y = 1.0 - a * a
    b = jnp.where(y > 0.0, y * lax.rsqrt(y), 0.0) * (i * u)

    n_tiles = tc // SUBLANES
    a = a.reshape(n_tiles, SUBLANES, width)
    b = b.reshape(n_tiles, SUBLANES, width)
    row = lax.broadcasted_iota(jnp.int32, a.shape, 1)
    d = 1
    while d < SUBLANES:
        if reverse:
            valid = row < SUBLANES - d
            shift = SUBLANES - d
        else:
            valid = row >= d
            shift = d
        a_sh = jnp.where(valid, pltpu.roll(a, shift, 1), 1.0)
        b_sh = jnp.where(valid, pltpu.roll(b, shift, 1), 0.0)
        b = a * b_sh + b
        a = a * a_sh
        d *= 2
    a = a.reshape(tc, width)
    b = b.reshape(tc, width)
    state = carry_ref[...]
    tiles = [None] * n_tiles
    for t in (range(n_tiles - 1, -1, -1) if reverse else range(n_tiles)):
        lo = t * SUBLANES
        tiles[t] = b[lo:lo + SUBLANES, :] + a[lo:lo + SUBLANES, :] * state
        state = tiles[t][0:1, :] if reverse else tiles[t][SUBLANES - 1:SUBLANES, :]
    carry_ref[...] = state
    h = jnp.concatenate(tiles, axis=0)
    h_out_ref[...] = h
    if combine:
        g = gate_ref[...]
        gelu = 0.5 * g * (1.0 + jnp.tanh(math.sqrt(2.0 / math.pi) * (g + 0.044715 * (g * g * g))))
        y_out_ref[...] = ((hf_ref[...] + h) * gelu).astype(y_out_ref.dtype)


def _lru_call(xl, conv_w, conv_b, w_gates, b_gates, lam, h0, hf=None, gate=None, *,
              seq, reverse, tc):
    t, width = xl.shape
    batch = t // seq
    n_chunks = seq // tc
    hb = tc // CONV_HALO
    n_halo = seq // CONV_HALO
    combine = hf is not None

    def chunk_of(j):
        return (n_chunks - 1 - j) if reverse else j

    def main_idx(b, j):
        return (b * n_chunks + chunk_of(j), 0)

    def prev_idx(b, j):
        return (b * n_halo + jnp.maximum(chunk_of(j) * hb - 1, 0), 0)

    def next_idx(b, j):
        return (b * n_halo + jnp.minimum((chunk_of(j) + 1) * hb, n_halo - 1), 0)

    in_specs = [pl.BlockSpec((CONV_HALO, width), prev_idx),
                pl.BlockSpec((tc, width), main_idx),
                pl.BlockSpec((CONV_HALO, width), next_idx),
                _full(conv_w.shape), _full((1, width)), _full(w_gates.shape),
                _full((1, 2 * width)), _full((1, width)),
                pl.BlockSpec((None, 1, width), lambda b, j: (b, 0, 0))]
    args = [xl, xl, xl, conv_w, conv_b.reshape(1, width), w_gates, b_gates.reshape(1, 2 * width),
            lam.reshape(1, width), h0.reshape(batch, 1, width)]
    out_shape = [jax.ShapeDtypeStruct((t, width), F32)]
    out_specs = [pl.BlockSpec((tc, width), main_idx)]
    if combine:
        in_specs += [pl.BlockSpec((tc, width), main_idx)] * 2
        args += [hf, gate]
        out_shape.append(jax.ShapeDtypeStruct((t, width), BF16))
        out_specs.append(pl.BlockSpec((tc, width), main_idx))
    res = pl.pallas_call(
        functools.partial(_lru_kernel, reverse=reverse, combine=combine, tc=tc, n_chunks=n_chunks),
        grid=(batch, n_chunks),
        in_specs=in_specs,
        out_specs=tuple(out_specs),
        out_shape=tuple(out_shape),
        scratch_shapes=[pltpu.VMEM((1, width), F32)],
        compiler_params=_cparams(("arbitrary", "arbitrary")),
        name="lru_bwd" if reverse else "lru_fwd",
    )(*args)
    return res if combine else res[0]


def _diff_lambda(dl_ref, lam_init):
    dl = dl_ref[...]
    s01 = jnp.sum(dl[0:1, :] * dl[1:2, :], axis=-1, keepdims=True)
    s23 = jnp.sum(dl[2:3, :] * dl[3:4, :], axis=-1, keepdims=True)
    return jnp.exp(s01) - jnp.exp(s23) + lam_init


def _diff_attn_kernel(q_ref, *refs, tq, n_blk, tk, n_kt, lam_init):
    if n_kt:
        o_ref = refs[-3]
        rest = refs[:-3] + refs[-2:]
    else:
        o_ref = refs[-1]
        rest = refs[:-1]
    for blk in range(n_blk):
        rows = slice(blk * tq, (blk + 1) * tq)
        o_ref[rows, :] = _diff_attn_block(q_ref[rows, :], *rest, tq=tq, tk=tk, n_kt=n_kt,
                                          lam_init=lam_init).astype(o_ref.dtype)


def _diff_attn_block(q, *refs, tq, tk, n_kt, lam_init):
    if n_kt:
        kc_ref, vc_ref, k_ref, v_ref, dl_ref, g_ref, s_a, s_b = refs
    else:
        kc_ref, vc_ref, dl_ref, g_ref = refs
    lane = lax.broadcasted_iota(jnp.int32, q.shape, 1)
    zero = jnp.zeros_like(q)
    qs = jnp.concatenate([jnp.where(lane < HEAD_DIM, q, zero), jnp.where(lane >= HEAD_DIM, q, zero)], axis=0)
    vw = vc_ref.shape[-1]

    def scores(kt):
        s = _dot_t(qs, kt)
        return s, jnp.max(s, axis=-1, keepdims=True)

    def update(load_s, vt, m_tile, m, acc):
        m_new = jnp.maximum(m, m_tile)
        acc = jnp.exp2(m - m_new) * acc
        v_ext = jnp.concatenate([vt, jnp.ones_like(vt)], axis=1)
        n_keys = vt.shape[0]
        kc = min(n_keys, MXU_K)
        for c0 in range(0, n_keys, kc):
            p = jnp.exp2(load_s(c0, kc) - m_new).astype(BF16)
            acc = acc + jnp.dot(p, v_ext[c0:c0 + kc, :], preferred_element_type=F32)
        return m_new, acc

    def k_tile(j):
        return k_ref[pl.ds(pl.multiple_of(j * tk, tk), tk), :]

    def v_tile(j):
        return v_ref[pl.ds(pl.multiple_of(j * tk, tk), tk), :]

    m = jnp.full((2 * tq, 1), -jnp.inf, F32)
    acc = jnp.zeros((2 * tq, 2 * vw), F32)
    if n_kt:
        s_first, mt_a = scores(k_tile(0))
        s_a[...] = s_first

        def pair(j, carry, last):
            m, acc, mt_a = carry
            s_next, mt_b = scores(k_tile(j + 1))
            s_b[...] = s_next
            m, acc = update(lambda c0, n: s_a[:, c0:c0 + n], v_tile(j), mt_a, m, acc)
            if last:
                s_ctx, mt_a = scores(kc_ref[...])
            else:
                s_ctx = None
                s_next, mt_a = scores(k_tile(j + 2))
                s_a[...] = s_next
            m, acc = update(lambda c0, n: s_b[:, c0:c0 + n], v_tile(j + 1), mt_b, m, acc)
            return (m, acc, mt_a), s_ctx

        carry = lax.fori_loop(0, n_kt // 2 - 1, lambda t, c: pair(2 * t, c, False)[0], (m, acc, mt_a))
        (m, acc, mt_c), s_c = pair(n_kt - 2, carry, True)
    else:
        s_c, mt_c = scores(kc_ref[...])
    m, acc = update(lambda c0, n: s_c[:, c0:c0 + n], vc_ref[...], mt_c, m, acc)
    o = acc[:, :vw] / acc[:, vw:vw + 1]
    lam = _diff_lambda(dl_ref, lam_init)
    od = o[:tq, :] - lam * o[tq:, :]
    return _rms(od, g_ref[...]) * (1.0 - lam_init)


def _diff_attn_call(q, kc, vc, k, v, dl, g, *, seq, n_ctx, tq, n_blk, tk, lam_init):
    t, width = q.shape
    batch = t // seq
    hw = 2 * HEAD_DIM
    heads = width // hw
    nq = seq // (tq * n_blk)
    kv_spec = lambda n: pl.BlockSpec((n, hw), lambda b, h, i: (b, h))
    q_spec = pl.BlockSpec((tq * n_blk, hw), lambda b, h, i: (b * nq + i, h))
    in_specs = [q_spec, kv_spec(n_ctx), kv_spec(n_ctx)]
    args = [q, kc, vc]
    n_kt = 0
    if k is not None:
        n_lat = k.shape[0] // batch
        n_kt = n_lat // tk
        in_specs += [kv_spec(n_lat), kv_spec(n_lat)]
        args += [k, v]
    in_specs += [_full(dl.shape), _full((1, hw))]
    args += [dl, g.reshape(1, hw)]
    assert n_kt % 2 == 0, n_kt
    scratch = [pltpu.VMEM((2 * tq, tk), F32)] * 2 if n_kt else []
    return pl.pallas_call(
        functools.partial(_diff_attn_kernel, tq=tq, n_blk=n_blk, tk=tk, n_kt=n_kt, lam_init=lam_init),
        grid=(batch, heads, nq),
        in_specs=in_specs,
        out_specs=q_spec,
        out_shape=jax.ShapeDtypeStruct((t, width), BF16),
        scratch_shapes=scratch,
        compiler_params=_cparams(("arbitrary", "arbitrary", "arbitrary")),
        name="diff_attn" if n_kt else "diff_attn_ctx",
    )(*args)


def _rpb_table_kernel(r_ref, o_ref):
    n = o_ref.shape[-1]
    j = lax.broadcasted_iota(jnp.int32, (r_ref.shape[-1], n), 0)
    pos = lax.broadcasted_iota(jnp.int32, (r_ref.shape[-1], n), 1)
    qc = pos // GRID_W
    kc = pos % GRID_W
    coff = jnp.clip(kc - qc + (NA_KW - 1), 0, 2 * NA_KW - 2)
    onehot = (coff == j).astype(F32)
    vals = jnp.dot(r_ref[...], onehot, preferred_element_type=F32, precision=lax.Precision.HIGHEST)
    pos1 = lax.broadcasted_iota(jnp.int32, vals.shape, 1)
    qc1 = pos1 // GRID_W
    kc1 = pos1 % GRID_W
    c0 = jnp.clip(qc1 - NA_KW // 2, 0, GRID_W - NA_KW)
    in_win = (kc1 >= c0) & (kc1 < c0 + NA_KW)
    o_ref[...] = jnp.where(in_win, vals * LOG2E, NEG_INF)


def _rpb_table(rpb):
    heads = rpb.shape[0]
    ncol = 2 * NA_KW
    rp = jnp.pad(rpb, ((0, 0), (0, 0), (0, ncol - rpb.shape[-1])))
    r_all = jnp.stack([rp[:, NA_KH - 1 - p:2 * NA_KH - 1 - p, :] for p in range(NA_KH)], axis=0)
    r_all = r_all.reshape(NA_KH, heads * NA_KH, ncol)
    n = GRID_W * GRID_W
    tab = pl.pallas_call(
        _rpb_table_kernel,
        grid=(NA_KH,),
        in_specs=[pl.BlockSpec((None, heads * NA_KH, ncol), lambda p: (p, 0, 0))],
        out_specs=pl.BlockSpec((None, heads * NA_KH, n), lambda p: (p, 0, 0)),
        out_shape=jax.ShapeDtypeStruct((NA_KH, heads * NA_KH, n), F32),
        compiler_params=_cparams(("arbitrary",)),
        name="rpb_table",
    )(r_all)
    tab = tab.reshape(NA_KH, heads, NA_KH, GRID_W, GRID_W).transpose(0, 1, 3, 2, 4)
    return tab.reshape(NA_KH, heads * GRID_W, NA_KH * GRID_W)


def _stack_heads(q, heads):
    head_of_lane = lax.broadcasted_iota(jnp.int32, q.shape, 1) // HEAD_DIM
    zero = jnp.zeros_like(q)
    return jnp.concatenate([jnp.where(head_of_lane == h, q, zero) for h in range(heads)], axis=0)


def _unstack_heads(res, heads):
    n = res.shape[0] // heads
    head_of_lane = lax.broadcasted_iota(jnp.int32, (n, res.shape[1]), 1) // HEAD_DIM
    out = jnp.zeros((n, res.shape[1]), res.dtype)
    for h in range(heads):
        out = jnp.where(head_of_lane == h, res[h * n:(h + 1) * n, :], out)
    return out


def _na_kernel(q_ref, k_ref, v_ref, kc_ref, vc_ref, bias_ref, o_ref, *, rows, heads, row_block):
    n_loc = NA_KH * GRID_W
    for i in range(row_block):
        r = pl.program_id(1) * row_block + i
        r0 = jnp.clip(r - NA_KH // 2, 0, rows - NA_KH)
        pat = r - r0
        start = pl.multiple_of(r0 * GRID_W, GRID_W)
        ks = k_ref[pl.ds(start, n_loc), :]
        vs = v_ref[pl.ds(start, n_loc), :]
        qs = _stack_heads(q_ref[i * GRID_W:(i + 1) * GRID_W, :], heads)
        s_loc = _dot_t(qs, ks) + bias_ref[pat]
        s_ctx = _dot_t(qs, kc_ref[...])
        m = jnp.maximum(jnp.max(s_loc, axis=-1, keepdims=True), jnp.max(s_ctx, axis=-1, keepdims=True))
        e_loc = jnp.exp2(s_loc - m)
        e_ctx = jnp.exp2(s_ctx - m)
        denom = jnp.sum(e_loc, axis=-1, keepdims=True) + jnp.sum(e_ctx, axis=-1, keepdims=True)
        res = (jnp.dot(e_loc.astype(BF16), vs, preferred_element_type=F32)
               + jnp.dot(e_ctx.astype(BF16), vc_ref[...], preferred_element_type=F32)) / denom
        o_ref[i * GRID_W:(i + 1) * GRID_W, :] = _unstack_heads(res, heads).astype(o_ref.dtype)


def _na_call(q, k, v, kc, vc, bias, *, seq, n_ctx, row_block):
    t, width = q.shape
    batch = t // seq
    rows = seq // GRID_W
    assert rows >= NA_KH and rows % row_block == 0, (rows, row_block)
    heads = width // HEAD_DIM
    steps = rows // row_block
    tq = row_block * GRID_W
    per_batch = lambda n: pl.BlockSpec((n, width), lambda b, r: (b, 0))
    return pl.pallas_call(
        functools.partial(_na_kernel, rows=rows, heads=heads, row_block=row_block),
        grid=(batch, steps),
        in_specs=[pl.BlockSpec((tq, width), lambda b, r: (b * steps + r, 0)),
                  per_batch(seq), per_batch(seq), per_batch(n_ctx), per_batch(n_ctx),
                  _full(bias.shape)],
        out_specs=pl.BlockSpec((tq, width), lambda b, r: (b * steps + r, 0)),
        out_shape=jax.ShapeDtypeStruct((t, width), BF16),
        compiler_params=_cparams(("arbitrary", "arbitrary")),
        name="na_attn",
    )(q, k, v, kc, vc, bias)


def _plain_attn_kernel(q_ref, k_ref, v_ref, o_ref, *, heads):
    qs = _stack_heads(q_ref[...], heads)
    s = _dot_t(qs, k_ref[...])
    e = jnp.exp2(s - jnp.max(s, axis=-1, keepdims=True))
    res = jnp.dot(e.astype(BF16), v_ref[...], preferred_element_type=F32) / jnp.sum(e, axis=-1, keepdims=True)
    o_ref[...] = _unstack_heads(res, heads).astype(o_ref.dtype)


def _plain_attn_call(q, k, v, *, seq):
    t, width = q.shape
    spec = pl.BlockSpec((seq, width), lambda b: (b, 0))
    return pl.pallas_call(
        functools.partial(_plain_attn_kernel, heads=width // HEAD_DIM),
        grid=(t // seq,),
        in_specs=[spec, spec, spec],
        out_specs=spec,
        out_shape=jax.ShapeDtypeStruct((t, width), BF16),
        compiler_params=_cparams(("arbitrary",)),
        name="ctx_plain_attn",
    )(q, k, v)


def _out_ffn_kernel(*refs, widths, hidden, h_chunk, final):
    (x_ref, lru_ref, diff_ref, na_ref, wo_ref, g1_ref, n2_ref, sh_ref, sc_ref, g2_ref,
     wgu_ref, wd_ref) = refs[:12]
    if final:
        fg_ref, o_ref = refs[12:]
    else:
        o_ref = refs[12]
    off = 0
    mix = None
    for part_ref, wd_ in zip((lru_ref, diff_ref, na_ref), widths):
        term = jnp.dot(part_ref[...], wo_ref[off:off + wd_, :], preferred_element_type=F32)
        mix = term if mix is None else mix + term
        off += wd_
    x1 = x_ref[...] + g1_ref[...] * mix
    hb = (_rms(x1, n2_ref[...]) * (1.0 + sc_ref[...]) + sh_ref[...]).astype(BF16)
    acc = None
    for c0 in range(0, hidden, h_chunk):
        c1 = min(c0 + h_chunk, hidden)
        g = jnp.dot(hb, wgu_ref[:, c0:c1], preferred_element_type=F32)
        u = jnp.dot(hb, wgu_ref[:, hidden + c0:hidden + c1], preferred_element_type=F32)
        a = (g * _sigmoid(g) * u).astype(BF16)
        term = jnp.dot(a, wd_ref[c0:c1, :], preferred_element_type=F32)
        acc = term if acc is None else acc + term
    x2 = x1 + g2_ref[...] * acc
    if final:
        x2 = _rms(x2, fg_ref[...])
    o_ref[...] = x2


def _out_ffn_call(xs, lru, diff, na, w_out, n2g, mods, w_gu, w_down, final_g,
                  *, layer, shared_row, rows_per_batch, tm, h_chunk):
    t, d = xs.shape
    hidden = w_down.shape[1]
    per_batch = rows_per_batch // tm
    mod = lambda k: _mod_spec(mods, layer, k, per_batch, shared_row)
    row = lambda wd_: pl.BlockSpec((tm, wd_), lambda i: (i, 0))
    widths = (lru.shape[1], diff.shape[1], na.shape[1])
    in_specs = [row(d), row(widths[0]), row(widths[1]), row(widths[2]), _layer(w_out, layer), mod(2),
                _layer(n2g, layer), mod(3), mod(4), mod(5), _layer(w_gu, layer), _layer(w_down, layer)]
    args = [xs, lru, diff, na, w_out, mods, n2g, mods, mods, mods, w_gu, w_down]
    final = final_g is not None
    if final:
        in_specs.append(_full((1, d)))
        args.append(final_g)
    return pl.pallas_call(
        functools.partial(_out_ffn_kernel, widths=widths, hidden=hidden, h_chunk=h_chunk, final=final),
        grid=(t // tm,),
        in_specs=in_specs,
        out_specs=row(d),
        out_shape=jax.ShapeDtypeStruct((t, d), F32),
        compiler_params=_cparams(("arbitrary",)),
        name="out_ffn_final" if final else "out_ffn",
    )(*args)


def _rope_tables(seq):
    rows = seq // GRID_W
    axis_dim = HEAD_DIM // 2
    inv_freq = 1.0 / (ROPE_BASE ** (jnp.arange(0, axis_dim, 2, dtype=F32) / axis_dim))
    ang_r = jnp.arange(rows, dtype=F32)[:, None] * inv_freq[None]
    ang_c = jnp.arange(GRID_W, dtype=F32)[:, None] * inv_freq[None]
    per_row = lambda a: jnp.broadcast_to(a[:, None, :], (rows, GRID_W, a.shape[-1])).reshape(seq, -1)
    per_col = lambda a: jnp.broadcast_to(a[None, :, :], (rows, GRID_W, a.shape[-1])).reshape(seq, -1)
    cos_r, sin_r = per_row(jnp.cos(ang_r)), per_row(jnp.sin(ang_r))
    cos_c, sin_c = per_col(jnp.cos(ang_c)), per_col(jnp.sin(ang_c))
    cos = jnp.concatenate([cos_r, cos_r, cos_c, cos_c], axis=-1)
    sin = jnp.concatenate([-sin_r, sin_r, -sin_c, sin_c], axis=-1)
    return jnp.concatenate([cos, cos], axis=-1), jnp.concatenate([sin, sin], axis=-1)


def _pick_tile(n, target):
    t = min(n, target)
    while n % t:
        t //= 2
    return t


def kernel(x, c, ctx, c_ctx, w_mod, b_mod, norm1_g, norm2_g, w_in, lru_conv_w, lru_conv_b, lru_wa, lru_ba,
           lru_wx, lru_bx, lru_lambda, diff_lambda, diff_subln_g, na_rpb, w_out, ffn_w_gu, ffn_w_down, final_g):
    batch, seq, d = x.shape
    n_ctx = ctx.shape[1]
    depth = w_mod.shape[0]
    lru_w = lru_conv_w.shape[-1]
    diff_w = d // 2
    na_w = d - lru_w - diff_w
    sections = (lru_w, lru_w, diff_w, diff_w, diff_w, na_w, na_w, na_w)
    assert sum(sections) == w_in.shape[-1], (sections, w_in.shape)
    hidden = ffn_w_down.shape[1]

    cvec = jnp.concatenate([c, c_ctx[None], jnp.zeros((MOD_ROWS - batch - 1, d), F32)], axis=0)
    mods = _mod_call(cvec, w_mod, b_mod).reshape(depth * MOD_ROWS * 6, 1, d)
    rope_tabs = _rope_tables(seq)

    w_in_b = w_in.astype(BF16)
    w_out_b = w_out.astype(BF16)
    w_gu_b = ffn_w_gu.astype(BF16)
    w_down_b = ffn_w_down.astype(BF16)
    n1g = norm1_g.reshape(depth, 1, d)
    n2g = norm2_g.reshape(depth, 1, d)
    fg = final_g.reshape(1, d)
    eye = jnp.eye(LRU_BLOCKS, dtype=F32)[None, None, :, None, :, None]
    dense = lambda w: (eye * w[:, :, :, :, None, :]).reshape(depth, 2, lru_w, lru_w)
    w_gates = jnp.concatenate([dense(lru_wa), dense(lru_wx)], axis=-1).astype(BF16)
    b_gates = jnp.concatenate([lru_ba, lru_bx], axis=-1)

    xs = x.reshape(batch * seq, d)
    hs = ctx.reshape(batch * n_ctx, d)
    tm = _pick_tile(seq, 512)
    tm_c = _pick_tile(n_ctx, 256)
    tc = _pick_tile(seq, 512)
    tc_c = _pick_tile(n_ctx, 256)
    zeros_h = jnp.zeros((batch, lru_w), F32)
    h_chunk = -(-hidden // (2 * MXU_K)) * MXU_K
    lat = dict(shared_row=None, rows_per_batch=seq, tm=tm)
    con = dict(shared_row=batch, rows_per_batch=n_ctx, tm=tm_c)

    for l in range(depth):
        last = l == depth - 1
        lam_init = 0.8 - 0.6 * math.exp(-0.3 * l)
        px = _in_proj_call(xs, n1g, mods, w_in_b, rope_tabs, layer=l, sections=sections,
                           **dict(lat, tm=_pick_tile(seq, 1024)))
        pc = _in_proj_call(hs, n1g, mods, w_in_b, None, layer=l, sections=sections, **con)
        xl, gl, qd, kd, vd, qn, kn, vn = px
        cxl, cgl, cqd, ckd, cvd, cqn, ckn, cvn = pc

        lru_f = (lru_conv_w[l], lru_conv_b[l], w_gates[l, 0], b_gates[l, 0], lru_lambda[l, 0])
        lru_b = (lru_conv_w[l], lru_conv_b[l], w_gates[l, 1], b_gates[l, 1], lru_lambda[l, 1])
        hc_f = _lru_call(cxl, *lru_f, zeros_h, seq=n_ctx, reverse=False, tc=tc_c)
        hc_b, lru_c = _lru_call(cxl, *lru_b, zeros_h, hc_f, cgl, seq=n_ctx, reverse=True, tc=tc_c)
        hcf_last = hc_f.reshape(batch, n_ctx, lru_w)[:, -1]
        hcb_first = hc_b.reshape(batch, n_ctx, lru_w)[:, 0]
        hl_f = _lru_call(xl, *lru_f, hcf_last, seq=seq, reverse=False, tc=tc)
        _, lru_x = _lru_call(xl, *lru_b, hcb_first, hl_f, gl, seq=seq, reverse=True, tc=tc)

        tq = _pick_tile(seq, 256)
        diff_x = _diff_attn_call(qd, ckd, cvd, kd, vd, diff_lambda[l], diff_subln_g[l], seq=seq, n_ctx=n_ctx,
                                 tq=tq, n_blk=min(4, seq // tq), tk=_pick_tile(seq // 4, 2048), lam_init=lam_init)

        bias = _rpb_table(na_rpb[l])
        na_x = _na_call(qn, kn, vn, ckn, cvn, bias, seq=seq, n_ctx=n_ctx, row_block=8)

        xs = _out_ffn_call(xs, lru_x, diff_x, na_x, w_out_b, n2g, mods, w_gu_b, w_down_b, fg if last else None,
                           layer=l, h_chunk=h_chunk, **lat)
        if not last:
            diff_c = _diff_attn_call(cqd, ckd, cvd, None, None, diff_lambda[l], diff_subln_g[l],
                                     seq=n_ctx, n_ctx=n_ctx, tq=n_ctx, n_blk=1, tk=n_ctx, lam_init=lam_init)
            na_c = _plain_attn_call(cqn, ckn, cvn, seq=n_ctx)
            hs = _out_ffn_call(hs, lru_c, diff_c, na_c, w_out_b, n2g, mods, w_gu_b, w_down_b, None,
                               layer=l, h_chunk=h_chunk, **con)
    return xs.reshape(batch, seq, d)
```

```python
import functools
import math

import jax
import jax.numpy as jnp
from jax import lax
from jax.experimental import pallas as pl
from jax.experimental.pallas import tpu as pltpu

GRID_W = 64
HEAD_DIM = 64
LRU_BLOCKS = 4
LRU_CONV = 4
LRU_C = 8.0
NA_KH = 8
NA_KW = 16
ROPE_BASE = 10000.0
NORM_EPS = 1e-6
NEG_INF = -1e30
LOG2E = math.log2(math.e)
MXU_K = 256
SUBLANES = 8
MOD_ROWS = 8
CONV_HALO = 8
VMEM_LIMIT = 56 * 1024 * 1024

F32 = jnp.float32
BF16 = jnp.bfloat16


def _cparams(sem):
    return pltpu.CompilerParams(dimension_semantics=sem, vmem_limit_bytes=VMEM_LIMIT)


def _full(shape):
    nd = len(shape)
    return pl.BlockSpec(shape, lambda *_: (0,) * nd, pipeline_mode=pl.Buffered(1))


def _layer(arr, l):
    tail = arr.shape[1:]
    return pl.BlockSpec((None,) + tail, lambda *_: (l,) + (0,) * len(tail), pipeline_mode=pl.Buffered(1))


def _mod_spec(mods, l, k, per_batch, shared_row):
    d = mods.shape[-1]
    base = (l * MOD_ROWS) * 6 + k
    if shared_row is not None:
        return pl.BlockSpec((None, 1, d), lambda i: (base + shared_row * 6, 0, 0))
    return pl.BlockSpec((None, 1, d), lambda i: (base + (i // per_batch) * 6, 0, 0))


def _rms(x, g):
    return x * lax.rsqrt(jnp.mean(x * x, axis=-1, keepdims=True) + NORM_EPS) * g


def _sigmoid(x):
    return 1.0 / (1.0 + jnp.exp(-x))


def _dot_t(a, b):
    return lax.dot_general(a, b, (((1,), (1,)), ((), ())), preferred_element_type=F32)


def _mod_kernel(c_ref, w_ref, b_ref, o_ref):
    cv = c_ref[...]
    cond = cv * _sigmoid(cv)
    o_ref[...] = jnp.dot(cond, w_ref[...], preferred_element_type=F32,
                         precision=lax.Precision.HIGHEST) + b_ref[...]


def _mod_call(cvec, w_mod, b_mod):
    depth, d, n = w_mod.shape
    rows = cvec.shape[0]
    tn = d
    return pl.pallas_call(
        _mod_kernel,
        grid=(depth, n // tn),
        in_specs=[pl.BlockSpec((rows, d), lambda l, j: (0, 0)),
                  pl.BlockSpec((None, d, tn), lambda l, j: (l, 0, j)),
                  pl.BlockSpec((None, 1, tn), lambda l, j: (l, 0, j))],
        out_specs=pl.BlockSpec((None, rows, tn), lambda l, j: (l, 0, j)),
        out_shape=jax.ShapeDtypeStruct((depth, rows, n), F32),
        compiler_params=_cparams(("arbitrary", "arbitrary")),
        name="mod",
    )(cvec, w_mod, b_mod.reshape(depth, 1, n))


def _rope(p, cos, sin_signed):
    n = p.shape[-1]
    lane = lax.broadcasted_iota(jnp.int32, p.shape, 1)
    low = (lane % 32) < 16
    partner = jnp.where(low, pltpu.roll(p, n - 16, 1), pltpu.roll(p, 16, 1))
    return p * cos + partner * sin_signed


def _in_proj_kernel(*refs, sections, rope, q_scale, sub_rows):
    if rope:
        x_ref, g_ref, sh_ref, sc_ref, w_ref, cos_ref, sin_ref = refs[:7]
        outs = refs[7:]
    else:
        x_ref, g_ref, sh_ref, sc_ref, w_ref = refs[:5]
        outs = refs[5:]
    for r0 in range(0, x_ref.shape[0], sub_rows):
        rows = slice(r0, r0 + sub_rows)
        h = _rms(x_ref[rows, :], g_ref[...]) * (1.0 + sc_ref[...]) + sh_ref[...]
        hb = h.astype(BF16)
        off = 0
        for idx, (width, o_ref) in enumerate(zip(sections, outs)):
            p = jnp.dot(hb, w_ref[:, off:off + width], preferred_element_type=F32)
            if rope and idx in (2, 3):
                reps = width // cos_ref.shape[-1]
                cos = jnp.concatenate([cos_ref[rows, :]] * reps, axis=-1)
                sin = jnp.concatenate([sin_ref[rows, :]] * reps, axis=-1)
                p = _rope(p, cos, sin)
            if idx in (2, 5):
                p = p * q_scale
            o_ref[rows, :] = p.astype(o_ref.dtype)
            off += width


def _in_proj_call(xs, g, mods, w, rope_tabs, *, layer, shared_row, sections, rows_per_batch, tm):
    t, d = xs.shape
    per_batch = rows_per_batch // tm
    in_specs = [pl.BlockSpec((tm, d), lambda i: (i, 0)),
                _layer(g, layer),
                _mod_spec(mods, layer, 0, per_batch, shared_row),
                _mod_spec(mods, layer, 1, per_batch, shared_row),
                _layer(w, layer)]
    args = [xs, g, mods, mods, w]
    rope = rope_tabs is not None
    if rope:
        cos, sin = rope_tabs
        tw = cos.shape[-1]
        in_specs += [pl.BlockSpec((tm, tw), lambda i: (i % per_batch, 0))] * 2
        args += [cos, sin]
    dtypes = (F32, F32, BF16, BF16, BF16, BF16, BF16, BF16)
    out_shape = tuple(jax.ShapeDtypeStruct((t, wd), dt) for wd, dt in zip(sections, dtypes))
    out_specs = tuple(pl.BlockSpec((tm, wd), lambda i: (i, 0)) for wd in sections)
    return pl.pallas_call(
        functools.partial(_in_proj_kernel, sections=sections, rope=rope, q_scale=HEAD_DIM ** -0.5 * LOG2E,
                          sub_rows=min(tm, 512)),
        grid=(t // tm,),
        in_specs=in_specs,
        out_specs=out_specs,
        out_shape=out_shape,
        compiler_params=_cparams(("arbitrary",)),
        name="in_proj_rope" if rope else "in_proj",
    )(*args)


def _lru_kernel(*refs, reverse, combine, tc, n_chunks):
    if combine:
        (prev_ref, x_ref, next_ref, cw_ref, cb_ref, w_ref, bias_ref, lam_ref, h0_ref,
         hf_ref, gate_ref, h_out_ref, y_out_ref, carry_ref) = refs
    else:
        (prev_ref, x_ref, next_ref, cw_ref, cb_ref, w_ref, bias_ref, lam_ref, h0_ref,
         h_out_ref, carry_ref) = refs
    j = pl.program_id(1)
    chunk = (n_chunks - 1 - j) if reverse else j

    @pl.when(j == 0)
    def _():
        carry_ref[...] = h0_ref[...]

    width = x_ref.shape[-1]
    has_prev = (chunk > 0).astype(F32)
    has_next = (chunk < n_chunks - 1).astype(F32)
    xcat = jnp.concatenate([prev_ref[...] * has_prev, x_ref[...], next_ref[...] * has_next], axis=0)
    left = LRU_CONV // 2
    u = cb_ref[...]
    for tap in range(LRU_CONV):
        start = CONV_HALO - left + tap
        u = u + cw_ref[tap:tap + 1, :] * xcat[start:start + tc, :]

    gates = jnp.dot(u.astype(BF16), w_ref[...], preferred_element_type=F32) + bias_ref[...]
    r = 0.5 + 0.5 * jnp.tanh(0.5 * gates[:, :width])
    i = 0.5 + 0.5 * jnp.tanh(0.5 * gates[:, width:])
    lam = lam_ref[...]
    log_sig = jnp.minimum(lam, 0.0) - jnp.log(1.0 + jnp.exp(-jnp.abs(lam)))
    log_a = LRU_C * r * log_sig
    a = jnp.exp(log_a)
    y = 1.0 - a * a
    b = jnp.where(y > 0.0, y * lax.rsqrt(y), 0.0) * (i * u)

    n_tiles = tc // SUBLANES
    a = a.reshape(n_tiles, SUBLANES, width)
    b = b.reshape(n_tiles, SUBLANES, width)
    row = lax.broadcasted_iota(jnp.int32, a.shape, 1)
    d = 1
    while d < SUBLANES:
        if reverse:
            valid = row < SUBLANES - d
            shift = SUBLANES - d
        else:
            valid = row >= d
            shift = d
        a_sh = jnp.where(valid, pltpu.roll(a, shift, 1), 1.0)
        b_sh = jnp.where(valid, pltpu.roll(b, shift, 1), 0.0)
        b = a * b_sh + b
        a = a * a_sh
        d *= 2
    a = a.reshape(tc, width)
    b = b.reshape(tc, width)
    state = carry_ref[...]
    tiles = [None] * n_tiles
    for t in (range(n_tiles - 1, -1, -1) if reverse else range(n_tiles)):
        lo = t * SUBLANES
        tiles[t] = b[lo:lo + SUBLANES, :] + a[lo:lo + SUBLANES, :] * state
        state = tiles[t][0:1, :] if reverse else tiles[t][SUBLANES - 1:SUBLANES, :]
    carry_ref[...] = state
    h = jnp.concatenate(tiles, axis=0)
    h_out_ref[...] = h
    if combine:
        g = gate_ref[...]
        gelu = 0.5 * g * (1.0 + jnp.tanh(math.sqrt(2.0 / math.pi) * (g + 0.044715 * (g * g * g))))
        y_out_ref[...] = ((hf_ref[...] + h) * gelu).astype(y_out_ref.dtype)


def _lru_call(xl, conv_w, conv_b, w_gates, b_gates, lam, h0, hf=None, gate=None, *,
              seq, reverse, tc):
    t, width = xl.shape
    batch = t // seq
    n_chunks = seq // tc
    hb = tc // CONV_HALO
    n_halo = seq // CONV_HALO
    combine = hf is not None

    def chunk_of(j):
        return (n_chunks - 1 - j) if reverse else j

    def main_idx(b, j):
        return (b * n_chunks + chunk_of(j), 0)

    def prev_idx(b, j):
        return (b * n_halo + jnp.maximum(chunk_of(j) * hb - 1, 0), 0)

    def next_idx(b, j):
        return (b * n_halo + jnp.minimum((chunk_of(j) + 1) * hb, n_halo - 1), 0)

    in_specs = [pl.BlockSpec((CONV_HALO, width), prev_idx),
                pl.BlockSpec((tc, width), main_idx),
                pl.BlockSpec((CONV_HALO, width), next_idx),
                _full(conv_w.shape), _full((1, width)), _full(w_gates.shape),
                _full((1, 2 * width)), _full((1, width)),
                pl.BlockSpec((None, 1, width), lambda b, j: (b, 0, 0))]
    args = [xl, xl, xl, conv_w, conv_b.reshape(1, width), w_gates, b_gates.reshape(1, 2 * width),
            lam.reshape(1, width), h0.reshape(batch, 1, width)]
    out_shape = [jax.ShapeDtypeStruct((t, width), F32)]
    out_specs = [pl.BlockSpec((tc, width), main_idx)]
    if combine:
        in_specs += [pl.BlockSpec((tc, width), main_idx)] * 2
        args += [hf, gate]
        out_shape.append(jax.ShapeDtypeStruct((t, width), BF16))
        out_specs.append(pl.BlockSpec((tc, width), main_idx))
    res = pl.pallas_call(
        functools.partial(_lru_kernel, reverse=reverse, combine=combine, tc=tc, n_chunks=n_chunks),
        grid=(batch, n_chunks),
        in_specs=in_specs,
        out_specs=tuple(out_specs),
        out_shape=tuple(out_shape),
        scratch_shapes=[pltpu.VMEM((1, width), F32)],
        compiler_params=_cparams(("arbitrary", "arbitrary")),
        name="lru_bwd" if reverse else "lru_fwd",
    )(*args)
    return res if combine else res[0]


def _diff_lambda(dl_ref, lam_init):
    dl = dl_ref[...]
    s01 = jnp.sum(dl[0:1, :] * dl[1:2, :], axis=-1, keepdims=True)
    s23 = jnp.sum(dl[2:3, :] * dl[3:4, :], axis=-1, keepdims=True)
    return jnp.exp(s01) - jnp.exp(s23) + lam_init


def _diff_attn_kernel(q_ref, *refs, tq, n_blk, tk, n_kt, lam_init):
    if n_kt:
        o_ref = refs[-3]
        rest = refs[:-3] + refs[-2:]
    else:
        o_ref = refs[-1]
        rest = refs[:-1]
    for blk in range(n_blk):
        rows = slice(blk * tq, (blk + 1) * tq)
        o_ref[rows, :] = _diff_attn_block(q_ref[rows, :], *rest, tq=tq, tk=tk, n_kt=n_kt,
                                          lam_init=lam_init).astype(o_ref.dtype)


def _diff_attn_block(q, *refs, tq, tk, n_kt, lam_init):
    if n_kt:
        kc_ref, vc_ref, k_ref, v_ref, dl_ref, g_ref, s_a, s_b = refs
    else:
        kc_ref, vc_ref, dl_ref, g_ref = refs
    lane = lax.broadcasted_iota(jnp.int32, q.shape, 1)
    zero = jnp.zeros_like(q)
    qs = jnp.concatenate([jnp.where(lane < HEAD_DIM, q, zero), jnp.where(lane >= HEAD_DIM, q, zero)], axis=0)
    vw = vc_ref.shape[-1]

    def scores(kt):
        s = _dot_t(qs, kt)
        return s, jnp.max(s, axis=-1, keepdims=True)

    def update(load_s, vt, m_tile, m, acc):
        m_new = jnp.maximum(m, m_tile)
        acc = jnp.exp2(m - m_new) * acc
        v_ext = jnp.concatenate([vt, jnp.ones_like(vt)], axis=1)
        n_keys = vt.shape[0]
        kc = min(n_keys, MXU_K)
        for c0 in range(0, n_keys, kc):
            p = jnp.exp2(load_s(c0, kc) - m_new).astype(BF16)
            acc = acc + jnp.dot(p, v_ext[c0:c0 + kc, :], preferred_element_type=F32)
        return m_new, acc

    def k_tile(j):
        return k_ref[pl.ds(pl.multiple_of(j * tk, tk), tk), :]

    def v_tile(j):
        return v_ref[pl.ds(pl.multiple_of(j * tk, tk), tk), :]

    m = jnp.full((2 * tq, 1), -jnp.inf, F32)
    acc = jnp.zeros((2 * tq, 2 * vw), F32)
    if n_kt:
        s_first, mt_a = scores(k_tile(0))
        s_a[...] = s_first

        def pair(j, carry, last):
            m, acc, mt_a = carry
            s_next, mt_b = scores(k_tile(j + 1))
            s_b[...] = s_next
            m, acc = update(lambda c0, n: s_a[:, c0:c0 + n], v_tile(j), mt_a, m, acc)
            if last:
                s_ctx, mt_a = scores(kc_ref[...])
            else:
                s_ctx = None
                s_next, mt_a = scores(k_tile(j + 2))
                s_a[...] = s_next
            m, acc = update(lambda c0, n: s_b[:, c0:c0 + n], v_tile(j + 1), mt_b, m, acc)
            return (m, acc, mt_a), s_ctx

        carry = lax.fori_loop(0, n_kt // 2 - 1, lambda t, c: pair(2 * t, c, False)[0], (m, acc, mt_a))
        (m, acc, mt_c), s_c = pair(n_kt - 2, carry, True)
    else:
        s_c, mt_c = scores(kc_ref[...])
    m, acc = update(lambda c0, n: s_c[:, c0:c0 + n], vc_ref[...], mt_c, m, acc)
    o = acc[:, :vw] / acc[:, vw:vw + 1]
    lam = _diff_lambda(dl_ref, lam_init)
    od = o[:tq, :] - lam * o[tq:, :]
    return _rms(od, g_ref[...]) * (1.0 - lam_init)


def _diff_attn_call(q, kc, vc, k, v, dl, g, *, seq, n_ctx, tq, n_blk, tk, lam_init):
    t, width = q.shape
    batch = t // seq
    hw = 2 * HEAD_DIM
    heads = width // hw
    nq = seq // (tq * n_blk)
    kv_spec = lambda n: pl.BlockSpec((n, hw), lambda b, h, i: (b, h))
    q_spec = pl.BlockSpec((tq * n_blk, hw), lambda b, h, i: (b * nq + i, h))
    in_specs = [q_spec, kv_spec(n_ctx), kv_spec(n_ctx)]
    args = [q, kc, vc]
    n_kt = 0
    if k is not None:
        n_lat = k.shape[0] // batch
        n_kt = n_lat // tk
        in_specs += [kv_spec(n_lat), kv_spec(n_lat)]
        args += [k, v]
    in_specs += [_full(dl.shape), _full((1, hw))]
    args += [dl, g.reshape(1, hw)]
    assert n_kt % 2 == 0, n_kt
    scratch = [pltpu.VMEM((2 * tq, tk), F32)] * 2 if n_kt else []
    return pl.pallas_call(
        functools.partial(_diff_attn_kernel, tq=tq, n_blk=n_blk, tk=tk, n_kt=n_kt, lam_init=lam_init),
        grid=(batch, heads, nq),
        in_specs=in_specs,
        out_specs=q_spec,
        out_shape=jax.ShapeDtypeStruct((t, width), BF16),
        scratch_shapes=scratch,
        compiler_params=_cparams(("arbitrary", "arbitrary", "arbitrary")),
        name="diff_attn" if n_kt else "diff_attn_ctx",
    )(*args)


def _rpb_table_kernel(r_ref, o_ref):
    n = o_ref.shape[-1]
    j = lax.broadcasted_iota(jnp.int32, (r_ref.shape[-1], n), 0)
    pos = lax.broadcasted_iota(jnp.int32, (r_ref.shape[-1], n), 1)
    qc = pos // GRID_W
    kc = pos % GRID_W
    coff = jnp.clip(kc - qc + (NA_KW - 1), 0, 2 * NA_KW - 2)
    onehot = (coff == j).astype(F32)
    vals = jnp.dot(r_ref[...], onehot, preferred_element_type=F32, precision=lax.Precision.HIGHEST)
    pos1 = lax.broadcasted_iota(jnp.int32, vals.shape, 1)
    qc1 = pos1 // GRID_W
    kc1 = pos1 % GRID_W
    c0 = jnp.clip(qc1 - NA_KW // 2, 0, GRID_W - NA_KW)
    in_win = (kc1 >= c0) & (kc1 < c0 + NA_KW)
    o_ref[...] = jnp.where(in_win, vals * LOG2E, NEG_INF)


def _rpb_table(rpb):
    heads = rpb.shape[0]
    ncol = 2 * NA_KW
    rp = jnp.pad(rpb, ((0, 0), (0, 0), (0, ncol - rpb.shape[-1])))
    r_all = jnp.stack([rp[:, NA_KH - 1 - p:2 * NA_KH - 1 - p, :] for p in range(NA_KH)], axis=0)
    r_all = r_all.reshape(NA_KH, heads * NA_KH, ncol)
    n = GRID_W * GRID_W
    tab = pl.pallas_call(
        _rpb_table_kernel,
        grid=(NA_KH,),
        in_specs=[pl.BlockSpec((None, heads * NA_KH, ncol), lambda p: (p, 0, 0))],
        out_specs=pl.BlockSpec((None, heads * NA_KH, n), lambda p: (p, 0, 0)),
        out_shape=jax.ShapeDtypeStruct((NA_KH, heads * NA_KH, n), F32),
        compiler_params=_cparams(("arbitrary",)),
        name="rpb_table",
    )(r_all)
    tab = tab.reshape(NA_KH, heads, NA_KH, GRID_W, GRID_W).transpose(0, 1, 3, 2, 4)
    return tab.reshape(NA_KH, heads * GRID_W, NA_KH * GRID_W)


def _stack_heads(q, heads):
    head_of_lane = lax.broadcasted_iota(jnp.int32, q.shape, 1) // HEAD_DIM
    zero = jnp.zeros_like(q)
    return jnp.concatenate([jnp.where(head_of_lane == h, q, zero) for h in range(heads)], axis=0)


def _unstack_heads(res, heads):
    n = res.shape[0] // heads
    head_of_lane = lax.broadcasted_iota(jnp.int32, (n, res.shape[1]), 1) // HEAD_DIM
    out = jnp.zeros((n, res.shape[1]), res.dtype)
    for h in range(heads):
        out = jnp.where(head_of_lane == h, res[h * n:(h + 1) * n, :], out)
    return out


def _na_kernel(q_ref, k_ref, v_ref, kc_ref, vc_ref, bias_ref, o_ref, *, rows, heads, row_block):
    n_loc = NA_KH * GRID_W
    for i in range(row_block):
        r = pl.program_id(1) * row_block + i
        r0 = jnp.clip(r - NA_KH // 2, 0, rows - NA_KH)
        pat = r - r0
        start = pl.multiple_of(r0 * GRID_W, GRID_W)
        ks = k_ref[pl.ds(start, n_loc), :]
        vs = v_ref[pl.ds(start, n_loc), :]
        qs = _stack_heads(q_ref[i * GRID_W:(i + 1) * GRID_W, :], heads)
        s_loc = _dot_t(qs, ks) + bias_ref[pat]
        s_ctx = _dot_t(qs, kc_ref[...])
        m = jnp.maximum(jnp.max(s_loc, axis=-1, keepdims=True), jnp.max(s_ctx, axis=-1, keepdims=True))
        e_loc = jnp.exp2(s_loc - m)
        e_ctx = jnp.exp2(s_ctx - m)
        denom = jnp.sum(e_loc, axis=-1, keepdims=True) + jnp.sum(e_ctx, axis=-1, keepdims=True)
        res = (jnp.dot(e_loc.astype(BF16), vs, preferred_element_type=F32)
               + jnp.dot(e_ctx.astype(BF16), vc_ref[...], preferred_element_type=F32)) / denom
        o_ref[i * GRID_W:(i + 1) * GRID_W, :] = _unstack_heads(res, heads).astype(o_ref.dtype)


def _na_call(q, k, v, kc, vc, bias, *, seq, n_ctx, row_block):
    t, width = q.shape
    batch = t // seq
    rows = seq // GRID_W
    assert rows >= NA_KH and rows % row_block == 0, (rows, row_block)
    heads = width // HEAD_DIM
    steps = rows // row_block
    tq = row_block * GRID_W
    per_batch = lambda n: pl.BlockSpec((n, width), lambda b, r: (b, 0))
    return pl.pallas_call(
        functools.partial(_na_kernel, rows=rows, heads=heads, row_block=row_block),
        grid=(batch, steps),
        in_specs=[pl.BlockSpec((tq, width), lambda b, r: (b * steps + r, 0)),
                  per_batch(seq), per_batch(seq), per_batch(n_ctx), per_batch(n_ctx),
                  _full(bias.shape)],
        out_specs=pl.BlockSpec((tq, width), lambda b, r: (b * steps + r, 0)),
        out_shape=jax.ShapeDtypeStruct((t, width), BF16),
        compiler_params=_cparams(("arbitrary", "arbitrary")),
        name="na_attn",
    )(q, k, v, kc, vc, bias)


def _plain_attn_kernel(q_ref, k_ref, v_ref, o_ref, *, heads):
    qs = _stack_heads(q_ref[...], heads)
    s = _dot_t(qs, k_ref[...])
    e = jnp.exp2(s - jnp.max(s, axis=-1, keepdims=True))
    res = jnp.dot(e.astype(BF16), v_ref[...], preferred_element_type=F32) / jnp.sum(e, axis=-1, keepdims=True)
    o_ref[...] = _unstack_heads(res, heads).astype(o_ref.dtype)


def _plain_attn_call(q, k, v, *, seq):
    t, width = q.shape
    spec = pl.BlockSpec((seq, width), lambda b: (b, 0))
    return pl.pallas_call(
        functools.partial(_plain_attn_kernel, heads=width // HEAD_DIM),
        grid=(t // seq,),
        in_specs=[spec, spec, spec],
        out_specs=spec,
        out_shape=jax.ShapeDtypeStruct((t, width), BF16),
        compiler_params=_cparams(("arbitrary",)),
        name="ctx_plain_attn",
    )(q, k, v)


def _out_ffn_kernel(*refs, widths, hidden, h_chunk, final):
    (x_ref, lru_ref, diff_ref, na_ref, wo_ref, g1_ref, n2_ref, sh_ref, sc_ref, g2_ref,
     wgu_ref, wd_ref) = refs[:12]
    if final:
        fg_ref, o_ref = refs[12:]
    else:
        o_ref = refs[12]
    off = 0
    mix = None
    for part_ref, wd_ in zip((lru_ref, diff_ref, na_ref), widths):
        term = jnp.dot(part_ref[...], wo_ref[off:off + wd_, :], preferred_element_type=F32)
        mix = term if mix is None else mix + term
        off += wd_
    x1 = x_ref[...] + g1_ref[...] * mix
    hb = (_rms(x1, n2_ref[...]) * (1.0 + sc_ref[...]) + sh_ref[...]).astype(BF16)
    acc = None
    for c0 in range(0, hidden, h_chunk):
        c1 = min(c0 + h_chunk, hidden)
        g = jnp.dot(hb, wgu_ref[:, c0:c1], preferred_element_type=F32)
        u = jnp.dot(hb, wgu_ref[:, hidden + c0:hidden + c1], preferred_element_type=F32)
        a = (g * _sigmoid(g) * u).astype(BF16)
        term = jnp.dot(a, wd_ref[c0:c1, :], preferred_element_type=F32)
        acc = term if acc is None else acc + term
    x2 = x1 + g2_ref[...] * acc
    if final:
        x2 = _rms(x2, fg_ref[...])
    o_ref[...] = x2


def _out_ffn_call(xs, lru, diff, na, w_out, n2g, mods, w_gu, w_down, final_g,
                  *, layer, shared_row, rows_per_batch, tm, h_chunk):
    t, d = xs.shape
    hidden = w_down.shape[1]
    per_batch = rows_per_batch // tm
    mod = lambda k: _mod_spec(mods, layer, k, per_batch, shared_row)
    row = lambda wd_: pl.BlockSpec((tm, wd_), lambda i: (i, 0))
    widths = (lru.shape[1], diff.shape[1], na.shape[1])
    in_specs = [row(d), row(widths[0]), row(widths[1]), row(widths[2]), _layer(w_out, layer), mod(2),
                _layer(n2g, layer), mod(3), mod(4), mod(5), _layer(w_gu, layer), _layer(w_down, layer)]
    args = [xs, lru, diff, na, w_out, mods, n2g, mods, mods, mods, w_gu, w_down]
    final = final_g is not None
    if final:
        in_specs.append(_full((1, d)))
        args.append(final_g)
    return pl.pallas_call(
        functools.partial(_out_ffn_kernel, widths=widths, hidden=hidden, h_chunk=h_chunk, final=final),
        grid=(t // tm,),
        in_specs=in_specs,
        out_specs=row(d),
        out_shape=jax.ShapeDtypeStruct((t, d), F32),
        compiler_params=_cparams(("arbitrary",)),
        name="out_ffn_final" if final else "out_ffn",
    )(*args)


def _rope_tables(seq):
    rows = seq // GRID_W
    axis_dim = HEAD_DIM // 2
    inv_freq = 1.0 / (ROPE_BASE ** (jnp.arange(0, axis_dim, 2, dtype=F32) / axis_dim))
    ang_r = jnp.arange(rows, dtype=F32)[:, None] * inv_freq[None]
    ang_c = jnp.arange(GRID_W, dtype=F32)[:, None] * inv_freq[None]
    per_row = lambda a: jnp.broadcast_to(a[:, None, :], (rows, GRID_W, a.shape[-1])).reshape(seq, -1)
    per_col = lambda a: jnp.broadcast_to(a[None, :, :], (rows, GRID_W, a.shape[-1])).reshape(seq, -1)
    cos_r, sin_r = per_row(jnp.cos(ang_r)), per_row(jnp.sin(ang_r))
    cos_c, sin_c = per_col(jnp.cos(ang_c)), per_col(jnp.sin(ang_c))
    cos = jnp.concatenate([cos_r, cos_r, cos_c, cos_c], axis=-1)
    sin = jnp.concatenate([-sin_r, sin_r, -sin_c, sin_c], axis=-1)
    return jnp.concatenate([cos, cos], axis=-1), jnp.concatenate([sin, sin], axis=-1)


def _pick_tile(n, target):
    t = min(n, target)
    while n % t:
        t //= 2
    return t


def kernel(x, c, ctx, c_ctx, w_mod, b_mod, norm1_g, norm2_g, w_in, lru_conv_w, lru_conv_b, lru_wa, lru_ba,
           lru_wx, lru_bx, lru_lambda, diff_lambda, diff_subln_g, na_rpb, w_out, ffn_w_gu, ffn_w_down, final_g):
    batch, seq, d = x.shape
    n_ctx = ctx.shape[1]
    depth = w_mod.shape[0]
    lru_w = lru_conv_w.shape[-1]
    diff_w = d // 2
    na_w = d - lru_w - diff_w
    sections = (lru_w, lru_w, diff_w, diff_w, diff_w, na_w, na_w, na_w)
    assert sum(sections) == w_in.shape[-1], (sections, w_in.shape)
    hidden = ffn_w_down.shape[1]

    cvec = jnp.concatenate([c, c_ctx[None], jnp.zeros((MOD_ROWS - batch - 1, d), F32)], axis=0)
    mods = _mod_call(cvec, w_mod, b_mod).reshape(depth * MOD_ROWS * 6, 1, d)
    rope_tabs = _rope_tables(seq)

    w_in_b = w_in.astype(BF16)
    w_out_b = w_out.astype(BF16)
    w_gu_b = ffn_w_gu.astype(BF16)
    w_down_b = ffn_w_down.astype(BF16)
    n1g = norm1_g.reshape(depth, 1, d)
    n2g = norm2_g.reshape(depth, 1, d)
    fg = final_g.reshape(1, d)
    eye = jnp.eye(LRU_BLOCKS, dtype=F32)[None, None, :, None, :, None]
    dense = lambda w: (eye * w[:, :, :, :, None, :]).reshape(depth, 2, lru_w, lru_w)
    w_gates = jnp.concatenate([dense(lru_wa), dense(lru_wx)], axis=-1).astype(BF16)
    b_gates = jnp.concatenate([lru_ba, lru_bx], axis=-1)

    xs = x.reshape(batch * seq, d)
    hs = ctx.reshape(batch * n_ctx, d)
    tm = _pick_tile(seq, 512)
    tm_c = _pick_tile(n_ctx, 256)
    tc = _pick_tile(seq, 512)
    tc_c = _pick_tile(n_ctx, 256)
    zeros_h = jnp.zeros((batch, lru_w), F32)
    h_chunk = -(-hidden // (2 * MXU_K)) * MXU_K
    lat = dict(shared_row=None, rows_per_batch=seq, tm=tm)
    con = dict(shared_row=batch, rows_per_batch=n_ctx, tm=tm_c)

    for l in range(depth):
        last = l == depth - 1
        lam_init = 0.8 - 0.6 * math.exp(-0.3 * l)
        px = _in_proj_call(xs, n1g, mods, w_in_b, rope_tabs, layer=l, sections=sections,
                           **dict(lat, tm=_pick_tile(seq, 1024)))
        pc = _in_proj_call(hs, n1g, mods, w_in_b, None, layer=l, sections=sections, **con)
        xl, gl, qd, kd, vd, qn, kn, vn = px
        cxl, cgl, cqd, ckd, cvd, cqn, ckn, cvn = pc

        lru_f = (lru_conv_w[l], lru_conv_b[l], w_gates[l, 0], b_gates[l, 0], lru_lambda[l, 0])
        lru_b = (lru_conv_w[l], lru_conv_b[l], w_gates[l, 1], b_gates[l, 1], lru_lambda[l, 1])
        hc_f = _lru_call(cxl, *lru_f, zeros_h, seq=n_ctx, reverse=False, tc=tc_c)
        hc_b, lru_c = _lru_call(cxl, *lru_b, zeros_h, hc_f, cgl, seq=n_ctx, reverse=True, tc=tc_c)
        hcf_last = hc_f.reshape(batch, n_ctx, lru_w)[:, -1]
        hcb_first = hc_b.reshape(batch, n_ctx, lru_w)[:, 0]
        hl_f = _lru_call(xl, *lru_f, hcf_last, seq=seq, reverse=False, tc=tc)
        _, lru_x = _lru_call(xl, *lru_b, hcb_first, hl_f, gl, seq=seq, reverse=True, tc=tc)

        tq = _pick_tile(seq, 256)
        diff_x = _diff_attn_call(qd, ckd, cvd, kd, vd, diff_lambda[l], diff_subln_g[l], seq=seq, n_ctx=n_ctx,
                                 tq=tq, n_blk=min(8, seq // tq), tk=_pick_tile(seq // 4, 2048), lam_init=lam_init)

        bias = _rpb_table(na_rpb[l])
        na_x = _na_call(qn, kn, vn, ckn, cvn, bias, seq=seq, n_ctx=n_ctx, row_block=8)

        xs = _out_ffn_call(xs, lru_x, diff_x, na_x, w_out_b, n2g, mods, w_gu_b, w_down_b, fg if last else None,
                           layer=l, h_chunk=h_chunk, **lat)
        if not last:
            diff_c = _diff_attn_call(cqd, ckd, cvd, None, None, diff_lambda[l], diff_subln_g[l],
                                     seq=n_ctx, n_ctx=n_ctx, tq=n_ctx, n_blk=1, tk=n_ctx, lam_init=lam_init)
            na_c = _plain_attn_call(cqn, ckn, cvn, seq=n_ctx)
            hs = _out_ffn_call(hs, lru_c, diff_c, na_c, w_out_b, n2g, mods, w_gu_b, w_down_b, None,
                               layer=l, h_chunk=h_chunk, **con)
    return xs.reshape(batch, seq, d)
```
